```python
import jax, jax.numpy as jnp
from jax import lax
import numpy as np

D_MODEL = 1024
BATCH = 8
SEQ = 4096
DEPTH = 4
DEC_BATCH = 16
DEC_SEQ = 32
PAST_LEN = 4096

CHUNK = 64
N_MIXERS = 3
N_A = (DEPTH + 2) // 3
N_B = (DEPTH + 1) // 3
N_C = DEPTH // 3
D_FF = 2816
EPS = 1e-6
RET_H = 8
RET_DK = 128
RET_DV = 256
RET_QK = RET_H * RET_DK
RET_V = RET_H * RET_DV
RET_IN = 2 * RET_QK + 2 * RET_V
ROPE_BASE = 10000.0
WINDOW = 128
WIN_CHUNKS = WINDOW // CHUNK
SWA_HQ = 16
SWA_HKV = 2
SWA_G = SWA_HQ // SWA_HKV
SWA_DH = 64
SWA_Q = SWA_HQ * SWA_DH
SWA_KV = SWA_HKV * SWA_DH
SWA_IN = SWA_Q + 2 * SWA_KV
HG_H = 8
HG_DK = 128
HG_DV = 128
HG_QK = HG_H * HG_DK
HG_V = HG_H * HG_DV
HG_IN = 2 * HG_QK + 2 * HG_V

kernel_name = "hybrid_chunk_causal_retention_swa_hgrn2_step"


def rmsnorm(x, g):
    xf = x.astype(jnp.float32)
    y = xf * lax.rsqrt(jnp.mean(xf * xf, axis=-1, keepdims=True) + EPS)
    return (y * g.astype(jnp.float32)).astype(x.dtype)


def head_rmsnorm(o, g):
    o = o * lax.rsqrt(jnp.mean(o * o, axis=-1, keepdims=True) + EPS)
    return o.reshape(o.shape[0], o.shape[1], -1) * g.astype(jnp.float32)


def swiglu(h, w_in, w_out):
    a, u = jnp.split(h @ w_in, 2, axis=-1)
    return (jax.nn.silu(a) * u) @ w_out


def rope(x, pos):
    half = x.shape[-1] // 2
    inv = ROPE_BASE ** (-jnp.arange(half, dtype=jnp.float32) / half)
    ang = pos[:, None] * inv[None, :]
    cos = jnp.cos(ang)[None, :, None, :]
    sin = jnp.sin(ang)[None, :, None, :]
    x1, x2 = x[..., :half], x[..., half:]
    return jnp.concatenate([x1 * cos - x2 * sin, x1 * sin + x2 * cos], axis=-1)


def chunk_scan(step, S0, seqs, chunk):
    b, L = seqs[0].shape[:2]
    n = L // chunk
    xs = tuple(jnp.moveaxis(a.reshape(b, n, chunk, *a.shape[2:]), 1, 0) for a in seqs)
    S, out = lax.scan(lambda S, xc: step(S, *xc), S0, xs)
    out = jnp.moveaxis(out, 0, 1)
    return S, out.reshape(b, L, *out.shape[3:])


def retention_log_decay():
    return jnp.log1p(-(2.0 ** (-5.0 - jnp.arange(RET_H, dtype=jnp.float32))))


def retention_chunk(S, q, k, v, log_gamma):
    L = q.shape[1]
    pos = jnp.arange(L, dtype=jnp.float32)
    dist = jnp.abs(pos[:, None] - pos[None, :])
    decay = jnp.exp(log_gamma[:, None, None] * dist)
    scores = jnp.einsum('bihd,bjhd->bhij', q, k) * decay[None]
    inner = jnp.einsum('bhij,bjhe->bihe', scores, v)
    cross_dec = jnp.exp(log_gamma[None, :] * (pos[:, None] + 1.0))
    cross = jnp.einsum('bihd,bhde->bihe', q, S) * cross_dec[None, :, :, None]
    k_dec = k * jnp.exp(log_gamma[None, :] * (L - 1.0 - pos[:, None]))[None, :, :, None]
    S_new = jnp.exp(log_gamma * L)[None, :, None, None] * S + jnp.einsum('bjhd,bjhe->bhde', k_dec, v)
    return S_new, inner + cross


def retention_mixer(h, S0, pos0, chunk, w_in, w_out, gn_g):
    b, L, _ = h.shape
    q, k, v, g = jnp.split(h @ w_in, [RET_QK, 2 * RET_QK, 2 * RET_QK + RET_V], axis=-1)
    pos = pos0 + jnp.arange(L, dtype=jnp.float32)
    q = rope(q.reshape(b, L, RET_H, RET_DK).astype(jnp.float32), pos)
    k = rope(k.reshape(b, L, RET_H, RET_DK).astype(jnp.float32), pos) * (RET_DK ** -0.5)
    v = v.reshape(b, L, RET_H, RET_DV).astype(jnp.float32)
    log_gamma = retention_log_decay()
    S, o = chunk_scan(lambda S, qc, kc, vc: retention_chunk(S, qc, kc, vc, log_gamma),
                      S0.astype(jnp.float32), (q, k, v), chunk)
    o = head_rmsnorm(o, gn_g)
    out = (jax.nn.silu(g.astype(jnp.float32)) * o) @ w_out
    return out.astype(h.dtype), S


def swa_project(h, w_in):
    b, L, _ = h.shape
    q, k, v = jnp.split(h @ w_in, [SWA_Q, SWA_Q + SWA_KV], axis=-1)
    return (q.reshape(b, L, SWA_HKV, SWA_G, SWA_DH), k.reshape(b, L, SWA_HKV, SWA_DH),
            v.reshape(b, L, SWA_HKV, SWA_DH))


def sink_attention(q, k, v, mask, sink):
    s = jnp.einsum('bnqhgd,bnkhd->bnhgqk', q, k).astype(jnp.float32) * (SWA_DH ** -0.5)
    s = jnp.where(mask[None, :, None, None], s, -jnp.inf)
    sk = sink.astype(jnp.float32).reshape(1, 1, SWA_HKV, SWA_G, 1, 1)
    m = jnp.maximum(s.max(axis=-1, keepdims=True), sk)
    p = jnp.exp(s - m)
    w = p / (p.sum(axis=-1, keepdims=True) + jnp.exp(sk - m))
    return jnp.einsum('bnhgqk,bnkhd->bnqhgd', w, v.astype(jnp.float32))


def swa_prompt(h, w_in, w_out, sink):
    b, L, _ = h.shape
    n = L // CHUNK
    q, k, v = swa_project(h, w_in)

    def band(a):
        ac = a.reshape(b, n, CHUNK, SWA_HKV, SWA_DH)
        ap = jnp.pad(ac, ((0, 0), (WIN_CHUNKS, 0), (0, 0), (0, 0), (0, 0)))
        return jnp.concatenate([ap[:, s:s + n] for s in range(WIN_CHUNKS + 1)], axis=2)

    slot_chunk = jnp.repeat(jnp.arange(WIN_CHUNKS + 1) - WIN_CHUNKS, CHUNK)
    key_chunk = jnp.arange(n)[:, None] + slot_chunk[None, :]
    mask = (key_chunk >= 0)[:, None, :]
    o = sink_attention(q.reshape(b, n, CHUNK, SWA_HKV, SWA_G, SWA_DH), band(k), band(v), mask, sink)
    out = o.reshape(b, L, SWA_Q) @ w_out
    return out.astype(h.dtype), k[:, L - WINDOW:], v[:, L - WINDOW:]


def swa_sample(h, k_cache, v_cache, w_in, w_out, sink):
    b, L, _ = h.shape
    q, k, v = swa_project(h, w_in)
    kk = jnp.concatenate([k_cache, k], axis=1)[:, None]
    vv = jnp.concatenate([v_cache, v], axis=1)[:, None]
    mask = jnp.ones((1, 1, k_cache.shape[1] + L), dtype=bool)
    o = sink_attention(q[:, None], kk, vv, mask, sink)
    out = o.reshape(b, L, SWA_Q) @ w_out
    return out.astype(h.dtype), k, v


def forget_lower_bounds(lb_param):
    c = jnp.cumsum(jax.nn.softmax(lb_param.astype(jnp.float32), axis=0), axis=0)
    return c - c[0]


def hgrn2_chunk(S, q, k, v, logf):
    L = q.shape[1]
    bcum = jnp.cumsum(logf, axis=1)
    tri = jnp.tril(jnp.ones((L, L), dtype=bool))
    diff = bcum[:, :, None] - bcum[:, None, :]
    decay = jnp.exp(jnp.where(tri[None, :, :, None, None], diff, -jnp.inf))
    scores = jnp.einsum('btshd,bshd->bhts', q[:, :, None] * decay, k)
    inner = jnp.einsum('bhts,bshe->bthe', scores, v)
    cross = jnp.einsum('bthd,bhde->bthe', q * jnp.exp(bcum), S)
    bL = bcum[:, -1]
    k_dec = k * jnp.exp(bL[:, None] - bcum)
    S_new = jnp.exp(bL)[..., None] * S + jnp.einsum('bshd,bshe->bhde', k_dec, v)
    return S_new, inner + cross


def hgrn2_mixer(h, S0, chunk, lb, w_in, w_out, gn_g):
    b, L, _ = h.shape
    q, f, i, g = jnp.split(h @ w_in, [HG_QK, 2 * HG_QK, 2 * HG_QK + HG_V], axis=-1)
    lb = lb.reshape(HG_H, HG_DK)
    f = f.reshape(b, L, HG_H, HG_DK).astype(jnp.float32)
    logf = jnp.log(lb + (1.0 - lb) * jax.nn.sigmoid(f))
    k = (1.0 - lb) * jax.nn.sigmoid(-f)
    q = jax.nn.silu(q.reshape(b, L, HG_H, HG_DK).astype(jnp.float32))
    v = i.reshape(b, L, HG_H, HG_DV).astype(jnp.float32)
    S, o = chunk_scan(hgrn2_chunk, S0.astype(jnp.float32), (q, k, v, logf), chunk)
    o = head_rmsnorm(o, gn_g)
    out = (jax.nn.silu(g.astype(jnp.float32)) * o) @ w_out
    return out.astype(h.dtype), S


def run_trunk(x, pos0, chunk, state_ret, cache_swa_k, cache_swa_v, state_hgrn, params):
    (norm_g, w_ff_in, w_ff_out, ret_w_in, ret_w_out, ret_gn_g, swa_w_in, swa_w_out, swa_sink,
     hg_w_in, hg_w_out, hg_gn_g, hg_lb) = params
    b = x.shape[0]
    has_past = state_ret is not None
    lbs = forget_lower_bounds(hg_lb)
    ret_out, k_out, v_out, hg_out = [], [], [], []
    for li in range(DEPTH):
        kind = li % N_MIXERS
        j = li // N_MIXERS
        x = x + 0.5 * rmsnorm(swiglu(rmsnorm(x, norm_g[li, 0]), w_ff_in[li, 0], w_ff_out[li, 0]), norm_g[li, 1])
        h = rmsnorm(x, norm_g[li, 2])
        if kind == 0:
            S0 = state_ret[j] if has_past else jnp.zeros((b, RET_H, RET_DK, RET_DV), jnp.float32)
            mix, S = retention_mixer(h, S0, pos0, chunk, ret_w_in[j], ret_w_out[j], ret_gn_g[j])
            ret_out.append(S)
        elif kind == 1:
            if has_past:
                mix, kn, vn = swa_sample(h, cache_swa_k[j], cache_swa_v[j], swa_w_in[j], swa_w_out[j], swa_sink[j])
            else:
                mix, kn, vn = swa_prompt(h, swa_w_in[j], swa_w_out[j], swa_sink[j])
            k_out.append(kn)
            v_out.append(vn)
        else:
            S0 = state_hgrn[j] if has_past else jnp.zeros((b, HG_H, HG_DK, HG_DV), jnp.float32)
            mix, S = hgrn2_mixer(h, S0, chunk, lbs[li], hg_w_in[j], hg_w_out[j], hg_gn_g[j])
            hg_out.append(S)
        x = x + rmsnorm(mix, norm_g[li, 3])
        x = x + 0.5 * rmsnorm(swiglu(rmsnorm(x, norm_g[li, 4]), w_ff_in[li, 1], w_ff_out[li, 1]), norm_g[li, 5])
    return x, jnp.stack(ret_out), jnp.stack(k_out), jnp.stack(v_out), jnp.stack(hg_out)


def setup_inputs(seed: int = 0) -> dict:
    key = jax.random.key(seed)
    ks = jax.random.split(key, 20)
    f32 = jnp.float32

    def nrm(k, shape, scale=1.0):
        return jax.random.normal(k, shape, f32) * scale

    swa_rows = min(WINDOW, PAST_LEN)
    return {
        "x_prompt": nrm(ks[0], (BATCH, SEQ, D_MODEL)),
        "x_sample": nrm(ks[1], (DEC_BATCH, DEC_SEQ, D_MODEL)),
        "state_ret": nrm(ks[2], (N_A, DEC_BATCH, RET_H, RET_DK, RET_DV)),
        "cache_swa_k": nrm(ks[3], (N_B, DEC_BATCH, swa_rows, SWA_HKV, SWA_DH)),
        "cache_swa_v": nrm(ks[4], (N_B, DEC_BATCH, swa_rows, SWA_HKV, SWA_DH)),
        "state_hgrn": nrm(ks[5], (N_C, DEC_BATCH, HG_H, HG_DK, HG_DV), 0.5),
        "norm_g": 1.0 + nrm(ks[6], (DEPTH, 6, D_MODEL), 0.05),
        "w_ff_in": nrm(ks[7], (DEPTH, 2, D_MODEL, 2 * D_FF), D_MODEL ** -0.5),
        "w_ff_out": nrm(ks[8], (DEPTH, 2, D_FF, D_MODEL), D_FF ** -0.5),
        "ret_w_in": nrm(ks[9], (N_A, D_MODEL, RET_IN), D_MODEL ** -0.5),
        "ret_w_out": nrm(ks[10], (N_A, RET_V, D_MODEL), RET_V ** -0.5),
        "ret_gn_g": 1.0 + nrm(ks[11], (N_A, RET_V), 0.05),
        "swa_w_in": nrm(ks[12], (N_B, D_MODEL, SWA_IN), D_MODEL ** -0.5),
        "swa_w_out": nrm(ks[13], (N_B, SWA_Q, D_MODEL), SWA_Q ** -0.5),
        "swa_sink": nrm(ks[14], (N_B, SWA_HQ), 0.5),
        "hg_w_in": nrm(ks[15], (N_C, D_MODEL, HG_IN), D_MODEL ** -0.5),
        "hg_w_out": nrm(ks[16], (N_C, HG_V, D_MODEL), HG_V ** -0.5),
        "hg_gn_g": 1.0 + nrm(ks[17], (N_C, HG_V), 0.05),
        "hg_lb": nrm(ks[18], (DEPTH, HG_QK), 0.5),
    }


def reference(x_prompt, x_sample, state_ret, cache_swa_k, cache_swa_v, state_hgrn, norm_g, w_ff_in, w_ff_out,
              ret_w_in, ret_w_out, ret_gn_g, swa_w_in, swa_w_out, swa_sink, hg_w_in, hg_w_out, hg_gn_g, hg_lb):
    params = (norm_g, w_ff_in, w_ff_out, ret_w_in, ret_w_out, ret_gn_g, swa_w_in, swa_w_out, swa_sink,
              hg_w_in, hg_w_out, hg_gn_g, hg_lb)
    y_prompt, ret_p, swa_k_p, swa_v_p, hg_p = run_trunk(x_prompt, 0, CHUNK, None, None, None, None, params)
    y_sample, ret_s, swa_k_s, swa_v_s, hg_s = run_trunk(x_sample, PAST_LEN, x_sample.shape[1], state_ret,
                                                        cache_swa_k, cache_swa_v, state_hgrn, params)
    return (y_prompt, y_sample, ret_p, ret_s, swa_k_p, swa_v_p, swa_k_s, swa_v_s, hg_p, hg_s)
```

```python
import functools

import jax
import jax.numpy as jnp
from jax import lax
from jax.experimental import pallas as pl
from jax.experimental.pallas import tpu as pltpu

F32 = jnp.float32
BF16 = jnp.bfloat16

D_MODEL = 1024
DEPTH = 4
CHUNK = 64
PAST_LEN = 4096
D_FF = 2816
EPS = 1e-6
RET_H, RET_DK, RET_DV = 8, 128, 256
RET_QK = RET_H * RET_DK
RET_V = RET_H * RET_DV
ROPE_BASE = 10000.0
WINDOW = 128
SWA_HQ, SWA_HKV, SWA_DH = 16, 2, 64
SWA_G = SWA_HQ // SWA_HKV
SWA_Q = SWA_HQ * SWA_DH
SWA_KV = SWA_HKV * SWA_DH
HG_H, HG_DK, HG_DV = 8, 128, 128
HG_QK = HG_H * HG_DK
HG_V = HG_H * HG_DV

VMEM_LIMIT_BYTES = 56 * 1024 * 1024
LANES = 128
MXU_DIM = 256
FFN_ROWS = 512
MIX_TOKENS = 256
HG_BLOCK = 16

NT_DIMS = (((1,), (1,)), ((), ()))
TN_DIMS = (((0,), (0,)), ((), ()))


def _rms(x, g):
    return x * lax.rsqrt(jnp.mean(x * x, axis=-1, keepdims=True) + EPS) * g


def _silu(x):
    return x * jax.nn.sigmoid(x)


def _dot(a, b):
    return jnp.dot(a, b, preferred_element_type=F32)


def _dot_nt(a, b):
    return lax.dot_general(a, b, NT_DIMS, preferred_element_type=F32)


def _dot_tn(a, b):
    return lax.dot_general(a, b, TN_DIMS, preferred_element_type=F32)


def _const_spec(shape):
    n = len(shape)
    return pl.BlockSpec(shape, lambda *_: (0,) * n, pipeline_mode=pl.Buffered(1))


def _params(n_axes):
    return pltpu.CompilerParams(
        dimension_semantics=("arbitrary",) * n_axes,
        vmem_limit_bytes=VMEM_LIMIT_BYTES)


def _ffn_kernel(x_ref, g_ref, win_ref, wout_ref, o_ref, act_ref):
    x = x_ref[...]
    h = _rms(x, g_ref[0:1, :]).astype(BF16)
    for c in range(D_FF // MXU_DIM):
        cols = slice(c * MXU_DIM, (c + 1) * MXU_DIM)
        ucols = slice(D_FF + c * MXU_DIM, D_FF + (c + 1) * MXU_DIM)
        a = _dot(h, win_ref[:, cols])
        u = _dot(h, win_ref[:, ucols])
        act_ref[:, cols] = (_silu(a) * u).astype(BF16)
    y = _dot(act_ref[...], wout_ref[...])
    o_ref[...] = x + 0.5 * _rms(y, g_ref[1:2, :])


def _ffn(x2d, g2, w_in, w_out):
    m = x2d.shape[0]
    tm = min(FFN_ROWS, m)
    return pl.pallas_call(
        _ffn_kernel,
        grid=(m // tm,),
        in_specs=[
            pl.BlockSpec((tm, D_MODEL), lambda i: (i, 0)),
            _const_spec((2, D_MODEL)),
            _const_spec((D_MODEL, 2 * D_FF)),
            _const_spec((D_FF, D_MODEL)),
        ],
        out_specs=pl.BlockSpec((tm, D_MODEL), lambda i: (i, 0)),
        out_shape=jax.ShapeDtypeStruct((m, D_MODEL), F32),
        scratch_shapes=[pltpu.VMEM((tm, D_FF), BF16)],
        compiler_params=_params(1),
        name="ffn",
    )(x2d, g2, w_in, w_out)


def _ret_kernel(*refs, t_tile, has_state):
    if has_state:
        (x_ref, g_ref, win_ref, wout_ref, gn_ref, rope_ref, dmat_ref, lg_ref, s0_ref,
         o_ref, s_ref, q_s, k_s, v_s, gate_s, mix_s) = refs
    else:
        (x_ref, g_ref, win_ref, wout_ref, gn_ref, rope_ref, dmat_ref, lg_ref,
         o_ref, s_ref, q_s, k_s, v_s, gate_s, mix_s) = refs
        s0_ref = None
    t = pl.program_id(1)

    @pl.when(t == 0)
    def _():
        if has_state:
            s_ref[...] = s0_ref[...]
        else:
            s_ref[...] = jnp.zeros_like(s_ref)

    x = x_ref[...]
    h = _rms(x, g_ref[0:1, :]).astype(BF16)
    q_s[...] = _dot(h, win_ref[:, 0:RET_QK])
    k_s[...] = _dot(h, win_ref[:, RET_QK:2 * RET_QK])
    v_s[...] = _dot(h, win_ref[:, 2 * RET_QK:2 * RET_QK + RET_V]).astype(BF16)
    gate_s[...] = _dot(h, win_ref[:, 2 * RET_QK + RET_V:])

    cq, sq, ck, sk = rope_ref[0], rope_ref[1], rope_ref[2], rope_ref[3]
    row = lax.broadcasted_iota(jnp.int32, (t_tile, LANES), 0).astype(F32)
    for hd in range(RET_H):
        qk = slice(hd * RET_DK, (hd + 1) * RET_DK)
        vv = slice(hd * RET_DV, (hd + 1) * RET_DV)
        lg = lg_ref[hd]
        lg_k = lg[:, :RET_DK]
        qh = q_s[:, qk]
        kh = k_s[:, qk]
        qr = qh * cq + pltpu.roll(qh, RET_DK // 2, 1) * sq
        kr = kh * ck + pltpu.roll(kh, RET_DK // 2, 1) * sk
        vh = v_s[:, vv]
        p = (_dot_nt(qr.astype(BF16), kr.astype(BF16)) * dmat_ref[hd]).astype(BF16)
        q_cross = (qr * jnp.exp(lg_k * (row + 1.0))).astype(BF16)
        k_dec = (kr * jnp.exp(lg_k * (float(t_tile - 1) - row))).astype(BF16)
        s_old = s_ref[hd]
        o = _dot(p, vh) + _dot(q_cross, s_old.astype(BF16))
        s_ref[hd] = jnp.exp(lg * float(t_tile)) * s_old + _dot_tn(k_dec, vh)
        o = _rms(o, gn_ref[:, vv])
        mix_s[:, vv] = (_silu(gate_s[:, vv]) * o).astype(BF16)
    y = _dot(mix_s[...], wout_ref[...])
    o_ref[...] = x + _rms(y, g_ref[1:2, :])


def _retention(x, g2, w_in, w_out, gn, rope_tab, dmat, lg_tab, s0, t_tile):
    b, l, _ = x.shape
    has_state = s0 is not None
    state_spec = pl.BlockSpec((None, RET_H, RET_DK, RET_DV), lambda i, j: (i, 0, 0, 0))
    in_specs = [
        pl.BlockSpec((None, t_tile, D_MODEL), lambda i, j: (i, j, 0)),
        _const_spec((2, D_MODEL)),
        _const_spec(w_in.shape),
        _const_spec(w_out.shape),
        _const_spec((1, RET_V)),
        pl.BlockSpec((4, t_tile, LANES), lambda i, j: (0, j, 0)),
        _const_spec(dmat.shape),
        _const_spec(lg_tab.shape),
    ]
    args = [x, g2, w_in, w_out, gn, rope_tab, dmat, lg_tab]
    if has_state:
        in_specs.append(state_spec)
        args.append(s0)
    return pl.pallas_call(
        functools.partial(_ret_kernel, t_tile=t_tile, has_state=has_state),
        grid=(b, l // t_tile),
        in_specs=in_specs,
        out_specs=[pl.BlockSpec((None, t_tile, D_MODEL), lambda i, j: (i, j, 0)), state_spec],
        out_shape=[jax.ShapeDtypeStruct(x.shape, F32),
                   jax.ShapeDtypeStruct((b, RET_H, RET_DK, RET_DV), F32)],
        scratch_shapes=[
            pltpu.VMEM((t_tile, RET_QK), F32),
            pltpu.VMEM((t_tile, RET_QK), F32),
            pltpu.VMEM((t_tile, RET_V), BF16),
            pltpu.VMEM((t_tile, RET_V), F32),
            pltpu.VMEM((t_tile, RET_V), BF16),
        ],
        compiler_params=_params(2),
        name="retention",
    )(*args)


def _swa_kernel(*refs, t_tile, chunk, has_cache):
    if has_cache:
        (x_ref, g_ref, win_ref, wout_ref, sink_ref, kc_ref, vc_ref,
         o_ref, ko_ref, vo_ref, q_s, kx_s, vx_s, att_s) = refs
    else:
        (x_ref, g_ref, win_ref, wout_ref, sink_ref,
         o_ref, ko_ref, vo_ref, q_s, kx_s, vx_s, att_s) = refs
    t = pl.program_id(1)
    n_keys = WINDOW + chunk
    n_out = ko_ref.shape[0]
    pairs = SWA_G // 2

    lane = lax.broadcasted_iota(jnp.int32, (1, LANES), 1)
    lo = lane < SWA_DH

    def expand(a):
        ar = pltpu.roll(a, SWA_DH, 1)
        zero = jnp.zeros_like(a)
        return (jnp.where(lo, a, zero).astype(BF16), jnp.where(lo, zero, ar).astype(BF16),
                jnp.where(lo, ar, zero).astype(BF16), jnp.where(lo, zero, a).astype(BF16))

    if has_cache:
        for i, a in enumerate(expand(kc_ref[...])):
            kx_s[i, 0:WINDOW, :] = a
        for i, a in enumerate(expand(vc_ref[...])):
            vx_s[i, 0:WINDOW, :] = a
    else:
        @pl.when(t == 0)
        def _():
            kx_s[:, 0:WINDOW, :] = jnp.zeros((4, WINDOW, LANES), BF16)
            vx_s[:, 0:WINDOW, :] = jnp.zeros((4, WINDOW, LANES), BF16)

    x = x_ref[...]
    h = _rms(x, g_ref[0:1, :]).astype(BF16)
    q_s[...] = _dot(h, win_ref[:, 0:SWA_Q]) * (SWA_DH ** -0.5)
    kv = _dot(h, win_ref[:, SWA_Q:])
    k_new = kv[:, 0:SWA_KV]
    v_new = kv[:, SWA_KV:]
    ko_ref[...] = k_new[t_tile - n_out:, :]
    vo_ref[...] = v_new[t_tile - n_out:, :]
    for i, a in enumerate(expand(k_new)):
        kx_s[i, WINDOW:, :] = a
    for i, a in enumerate(expand(v_new)):
        vx_s[i, WINDOW:, :] = a

    first_valid = jnp.where(t == 0, WINDOW, 0)
    col = lax.broadcasted_iota(jnp.int32, (1, n_keys), 1)
    for c in range(t_tile // chunk):
        rows = slice(c * chunk, (c + 1) * chunk)
        keys = slice(c * chunk, c * chunk + n_keys)
        for grp in range(SWA_HKV):
            qst = jnp.concatenate(
                [q_s[rows, (grp * pairs + p) * LANES:(grp * pairs + p + 1) * LANES]
                 for p in range(pairs)], axis=0).astype(BF16)
            out = None
            for par in range(2):
                idx = grp * 2 + par
                s = _dot_nt(qst, kx_s[idx, keys, :])
                if (not has_cache) and c * chunk < WINDOW:
                    s = jnp.where(col + c * chunk >= first_valid, s, -jnp.inf)
                snk = sink_ref[idx][:, 0:1]
                m = jnp.maximum(jnp.max(s, axis=-1, keepdims=True), snk)
                e = jnp.exp(s - m)
                den = jnp.sum(e, axis=-1, keepdims=True) + jnp.exp(snk - m)
                w = (e / den).astype(BF16)
                pv = _dot(w, vx_s[idx, keys, :])
                out = pv if out is None else out + pv
            for p in range(pairs):
                cols = slice((grp * pairs + p) * LANES, (grp * pairs + p + 1) * LANES)
                att_s[rows, cols] = out[p * chunk:(p + 1) * chunk, :].astype(BF16)

    if not has_cache:
        kx_s[:, 0:WINDOW, :] = kx_s[:, t_tile:t_tile + WINDOW, :]
        vx_s[:, 0:WINDOW, :] = vx_s[:, t_tile:t_tile + WINDOW, :]

    y = _dot(att_s[...], wout_ref[...])
    o_ref[...] = x + _rms(y, g_ref[1:2, :])


def _swa(x, g2, w_in, w_out, sink_tab, k_cache, v_cache, t_tile, chunk, n_out):
    b, l, _ = x.shape
    has_cache = k_cache is not None
    in_specs = [
        pl.BlockSpec((None, t_tile, D_MODEL), lambda i, j: (i, j, 0)),
        _const_spec((2, D_MODEL)),
        _const_spec(w_in.shape),
        _const_spec(w_out.shape),
        _const_spec(sink_tab.shape),
    ]
    args = [x, g2, w_in, w_out, sink_tab]
    if has_cache:
        cache_spec = pl.BlockSpec((None, WINDOW, SWA_KV), lambda i, j: (i, 0, 0))
        in_specs += [cache_spec, cache_spec]
        args += [k_cache, v_cache]
    kv_out = pl.BlockSpec((None, n_out, SWA_KV), lambda i, j: (i, 0, 0))
    return pl.pallas_call(
        functools.partial(_swa_kernel, t_tile=t_tile, chunk=chunk, has_cache=has_cache),
        grid=(b, l // t_tile),
        in_specs=in_specs,
        out_specs=[pl.BlockSpec((None, t_tile, D_MODEL), lambda i, j: (i, j, 0)), kv_out, kv_out],
        out_shape=[jax.ShapeDtypeStruct(x.shape, F32),
                   jax.ShapeDtypeStruct((b, n_out, SWA_KV), F32),
                   jax.ShapeDtypeStruct((b, n_out, SWA_KV), F32)],
        scratch_shapes=[
            pltpu.VMEM((t_tile, SWA_Q), F32),
            pltpu.VMEM((4, WINDOW + t_tile, LANES), BF16),
            pltpu.VMEM((4, WINDOW + t_tile, LANES), BF16),
            pltpu.VMEM((t_tile, SWA_Q), BF16),
        ],
        compiler_params=_params(2),
        name="swa",
    )(*args)


def _split3(a):
    hi = a.astype(BF16)
    r = a - hi.astype(F32)
    mid = r.astype(BF16)
    low = (r - mid.astype(F32)).astype(BF16)
    return hi, mid, low


def _hgrn_kernel(*refs, t_tile, has_state):
    if has_state:
        (x_ref, g_ref, win_ref, wout_ref, gn_ref, lb_ref, tri_ref, s0_ref,
         o_ref, s_ref, st_s, q_s, k_s, v_s, b_s, gate_s, o_s, qcat_s, kcat_s, mix_s) = refs
    else:
        (x_ref, g_ref, win_ref, wout_ref, gn_ref, lb_ref, tri_ref,
         o_ref, s_ref, st_s, q_s, k_s, v_s, b_s, gate_s, o_s, qcat_s, kcat_s, mix_s) = refs
    t = pl.program_id(1)
    nb = t_tile // HG_BLOCK

    @pl.when(t == 0)
    def _():
        for hd in range(HG_H):
            if has_state:
                st_s[hd] = s0_ref[hd].T
            else:
                st_s[hd] = jnp.zeros((HG_DV, HG_DK), F32)

    x = x_ref[...]
    h = _rms(x, g_ref[0:1, :]).astype(BF16)
    lb = lb_ref[...]
    q_s[...] = _silu(_dot(h, win_ref[:, 0:HG_QK]))
    f = _dot(h, win_ref[:, HG_QK:2 * HG_QK])
    k_s[...] = (1.0 - lb) * jax.nn.sigmoid(-f)
    logf = jnp.log(lb + (1.0 - lb) * jax.nn.sigmoid(f))
    v_s[...] = _dot(h, win_ref[:, 2 * HG_QK:2 * HG_QK + HG_V]).astype(BF16)
    gate_s[...] = _dot(h, win_ref[:, 2 * HG_QK + HG_V:])
    tri = tri_ref[...]
    hi, mid, low = _split3(logf)
    b_s[...] = _dot(tri, hi) + _dot(tri, mid) + _dot(tri, low)

    qcat_s[...] = jnp.zeros_like(qcat_s)
    kcat_s[...] = jnp.zeros_like(kcat_s)

    for hd in range(HG_H):
        ln = slice(hd * LANES, (hd + 1) * LANES)
        q = q_s[:, ln]
        k = k_s[:, ln]
        bc = b_s[:, ln]
        v = v_s[:, ln]
        for j in range(nb - 1):
            blk = slice(j * HG_BLOCK, (j + 1) * HG_BLOCK)
            after = slice((j + 1) * HG_BLOCK, t_tile)
            cols = slice(j * LANES, (j + 1) * LANES)
            e_j = bc[(j + 1) * HG_BLOCK - 1:(j + 1) * HG_BLOCK, :]
            kcat_s[blk, cols] = (k[blk] * jnp.exp(e_j - bc[blk])).astype(BF16)
            qcat_s[after, cols] = (q[after] * jnp.exp(bc[after] - e_j)).astype(BF16)
        st_old = st_s[hd]
        o = _dot_nt((q * jnp.exp(bc)).astype(BF16), st_old.astype(BF16))
        if nb > 1:
            a_off = _dot_nt(qcat_s[...], kcat_s[...])
            o = o + _dot(a_off.astype(BF16), v)
        o_s[:, ln] = o
        b_last = bc[t_tile - 1:t_tile, :]
        k_end = (k * jnp.exp(b_last - bc)).astype(BF16)
        st_new = st_old * jnp.exp(b_last) + _dot_tn(v, k_end)
        st_s[hd] = st_new
        s_ref[hd] = st_new.T

        def diag(j, carry, ln=ln):
            r0 = pl.multiple_of(j * HG_BLOCK, HG_BLOCK)
            rows = pl.ds(r0, HG_BLOCK)
            qb = q_s[rows, ln]
            kb = k_s[rows, ln]
            bb = b_s[rows, ln]
            vb = v_s[rows, ln].astype(F32)
            tt = lax.broadcasted_iota(jnp.int32, (HG_BLOCK, LANES), 0)
            acc = jnp.zeros((HG_BLOCK, LANES), F32)
            for s in range(HG_BLOCK):
                d = jnp.where(tt >= s, bb - bb[s:s + 1, :], -jnp.inf)
                a = jnp.sum(qb * kb[s:s + 1, :] * jnp.exp(d), axis=-1, keepdims=True)
                acc = acc + a * vb[s:s + 1, :]
            o_s[rows, ln] = o_s[rows, ln] + acc
            return carry

        lax.fori_loop(0, nb, diag, 0)

    for hd in range(HG_H):
        ln = slice(hd * LANES, (hd + 1) * LANES)
        o = _rms(o_s[:, ln], gn_ref[:, ln])
        mix_s[:, ln] = (_silu(gate_s[:, ln]) * o).astype(BF16)
    y = _dot(mix_s[...], wout_ref[...])
    o_ref[...] = x + _rms(y, g_ref[1:2, :])


def _hgrn(x, g2, w_in, w_out, gn, lb, s0, t_tile):
    b, l, _ = x.shape
    has_state = s0 is not None
    nb = t_tile // HG_BLOCK
    tri = jnp.tril(jnp.ones((t_tile, t_tile), F32)).astype(BF16)
    state_spec = pl.BlockSpec((None, HG_H, HG_DK, HG_DV), lambda i, j: (i, 0, 0, 0))
    in_specs = [
        pl.BlockSpec((None, t_tile, D_MODEL), lambda i, j: (i, j, 0)),
        _const_spec((2, D_MODEL)),
        _const_spec(w_in.shape),
        _const_spec(w_out.shape),
        _const_spec((1, HG_V)),
        _const_spec((1, HG_QK)),
        _const_spec((t_tile, t_tile)),
    ]
    args = [x, g2, w_in, w_out, gn, lb, tri]
    if has_state:
        in_specs.append(state_spec)
        args.append(s0)
    cat_cols = max(nb - 1, 1) * LANES
    return pl.pallas_call(
        functools.partial(_hgrn_kernel, t_tile=t_tile, has_state=has_state),
        grid=(b, l // t_tile),
        in_specs=in_specs,
        out_specs=[pl.BlockSpec((None, t_tile, D_MODEL), lambda i, j: (i, j, 0)), state_spec],
        out_shape=[jax.ShapeDtypeStruct(x.shape, F32),
                   jax.ShapeDtypeStruct((b, HG_H, HG_DK, HG_DV), F32)],
        scratch_shapes=[
            pltpu.VMEM((HG_H, HG_DV, HG_DK), F32),
            pltpu.VMEM((t_tile, HG_QK), F32),
            pltpu.VMEM((t_tile, HG_QK), F32),
            pltpu.VMEM((t_tile, HG_V), BF16),
            pltpu.VMEM((t_tile, HG_QK), F32),
            pltpu.VMEM((t_tile, HG_V), F32),
            pltpu.VMEM((t_tile, HG_V), F32),
            pltpu.VMEM((t_tile, cat_cols), BF16),
            pltpu.VMEM((t_tile, cat_cols), BF16),
            pltpu.VMEM((t_tile, HG_V), BF16),
        ],
        compiler_params=_params(2),
        name="hgrn2",
    )(*args)


def _rope_table(pos0, length):
    half = RET_DK // 2
    inv = ROPE_BASE ** (-jnp.arange(half, dtype=F32) / half)
    pos = pos0 + jnp.arange(length, dtype=F32)
    ang = pos[:, None] * inv[None, :]
    cos = jnp.cos(ang)
    sin = jnp.sin(ang)
    c2 = jnp.concatenate([cos, cos], axis=-1)
    s2 = jnp.concatenate([-sin, sin], axis=-1)
    ks = RET_DK ** -0.5
    return jnp.stack([c2, s2, c2 * ks, s2 * ks])


def _ret_log_decay():
    return jnp.log1p(-(2.0 ** (-5.0 - jnp.arange(RET_H, dtype=F32))))


def _ret_decay_matrix(t_tile, chunk):
    lg = _ret_log_decay()
    n = jnp.arange(t_tile)
    cn = n // chunk
    diff = (n[:, None] - n[None, :]).astype(F32)
    same = cn[:, None] == cn[None, :]
    later = cn[:, None] > cn[None, :]
    expo = jnp.where(same, jnp.abs(diff), diff)
    dec = jnp.exp(lg[:, None, None] * expo[None])
    return jnp.where((same | later)[None], dec, 0.0)


def _sink_table(sink, chunk):
    s = sink.astype(F32).reshape(SWA_HKV, SWA_G // 2, 2)
    s = jnp.transpose(s, (0, 2, 1)).reshape(SWA_HKV * 2, SWA_G // 2)
    s = jnp.repeat(s, chunk, axis=1)
    return jnp.broadcast_to(s[:, :, None], s.shape + (LANES,))


def _lower_bounds(lb_param):
    c = jnp.cumsum(jax.nn.softmax(lb_param.astype(F32), axis=0), axis=0)
    return c - c[0]


def _trunk(x, pos0, chunk, t_tile, state_ret, cache_k, cache_v, state_hgrn, params):
    (norm_g, w_ff_in, w_ff_out, ret_w_in, ret_w_out, ret_gn_g, swa_w_in, swa_w_out, swa_sink,
     hg_w_in, hg_w_out, hg_gn_g, lbs) = params
    b, l, _ = x.shape
    has_past = state_ret is not None
    rope_tab = _rope_table(pos0, l)
    dmat = _ret_decay_matrix(t_tile, chunk)
    lg_tab = jnp.broadcast_to(_ret_log_decay()[:, None, None], (RET_H, 1, RET_DV))
    ret_out, k_out, v_out, hg_out = [], [], [], []

    def ffn(x, li, which):
        g2 = norm_g[li, 4 * which:4 * which + 2]
        y = _ffn(x.reshape(b * l, D_MODEL), g2, w_ff_in[li, which], w_ff_out[li, which])
        return y.reshape(b, l, D_MODEL)

    for li in range(DEPTH):
        kind, j = li % 3, li // 3
        x = ffn(x, li, 0)
        g2 = norm_g[li, 2:4]
        if kind == 0:
            x, s_new = _retention(x, g2, ret_w_in[j], ret_w_out[j], ret_gn_g[j][None, :], rope_tab,
                                  dmat, lg_tab, state_ret[j] if has_past else None, t_tile)
            ret_out.append(s_new)
        elif kind == 1:
            sink_tab = _sink_table(swa_sink[j], chunk)
            if has_past:
                kc = cache_k[j].reshape(b, WINDOW, SWA_KV)
                vc = cache_v[j].reshape(b, WINDOW, SWA_KV)
                x, kn, vn = _swa(x, g2, swa_w_in[j], swa_w_out[j], sink_tab, kc, vc, t_tile, chunk, l)
            else:
                x, kn, vn = _swa(x, g2, swa_w_in[j], swa_w_out[j], sink_tab, None, None,
                                 t_tile, chunk, WINDOW)
            k_out.append(kn.reshape(b, -1, SWA_HKV, SWA_DH))
            v_out.append(vn.reshape(b, -1, SWA_HKV, SWA_DH))
        else:
            x, s_new = _hgrn(x, g2, hg_w_in[j], hg_w_out[j], hg_gn_g[j][None, :], lbs[li][None, :],
                             state_hgrn[j] if has_past else None, t_tile)
            hg_out.append(s_new)
        x = ffn(x, li, 1)
    return x, jnp.stack(ret_out), jnp.stack(k_out), jnp.stack(v_out), jnp.stack(hg_out)


def kernel(x_prompt, x_sample, state_ret, cache_swa_k, cache_swa_v, state_hgrn, norm_g, w_ff_in, w_ff_out,
           ret_w_in, ret_w_out, ret_gn_g, swa_w_in, swa_w_out, swa_sink, hg_w_in, hg_w_out, hg_gn_g, hg_lb):
    bf = lambda w: w.astype(BF16)
    params = (norm_g, bf(w_ff_in), bf(w_ff_out), bf(ret_w_in), bf(ret_w_out), ret_gn_g,
              bf(swa_w_in), bf(swa_w_out), swa_sink, bf(hg_w_in), bf(hg_w_out), hg_gn_g,
              _lower_bounds(hg_lb))
    y_p, ret_p, k_p, v_p, hg_p = _trunk(x_prompt, 0.0, CHUNK, MIX_TOKENS, None, None, None, None, params)
    dec_len = x_sample.shape[1]
    y_s, ret_s, k_s, v_s, hg_s = _trunk(x_sample, float(PAST_LEN), dec_len, dec_len, state_ret,
                                        cache_swa_k, cache_swa_v, state_hgrn, params)
    return (y_p, y_s, ret_p, ret_s, k_p, v_p, k_s, v_s, hg_p, hg_s)
```

```python
import functools

import jax
import jax.numpy as jnp
from jax import lax
from jax.experimental import pallas as pl
from jax.experimental.pallas import tpu as pltpu

F32 = jnp.float32
BF16 = jnp.bfloat16

D_MODEL = 1024
DEPTH = 4
CHUNK = 64
PAST_LEN = 4096
D_FF = 2816
EPS = 1e-6
RET_H, RET_DK, RET_DV = 8, 128, 256
RET_QK = RET_H * RET_DK
RET_V = RET_H * RET_DV
ROPE_BASE = 10000.0
WINDOW = 128
SWA_HQ, SWA_HKV, SWA_DH = 16, 2, 64
SWA_G = SWA_HQ // SWA_HKV
SWA_Q = SWA_HQ * SWA_DH
SWA_KV = SWA_HKV * SWA_DH
HG_H, HG_DK, HG_DV = 8, 128, 128
HG_QK = HG_H * HG_DK
HG_V = HG_H * HG_DV

VMEM_LIMIT_BYTES = 56 * 1024 * 1024
LANES = 128
SUBLANES = 8
MXU_DIM = 256
FFN_ROWS = 512
MIX_TOKENS = 256
HG_CHUNK = 64
HG_SUB = SUBLANES

NT_DIMS = (((1,), (1,)), ((), ()))
TN_DIMS = (((0,), (0,)), ((), ()))


def _rms(x, g):
    return x * lax.rsqrt(jnp.mean(x * x, axis=-1, keepdims=True) + EPS) * g


def _silu(x):
    return x * jax.nn.sigmoid(x)


def _dot(a, b):
    return jnp.dot(a, b, preferred_element_type=F32)


def _dot_nt(a, b):
    return lax.dot_general(a, b, NT_DIMS, preferred_element_type=F32)


def _dot_tn(a, b):
    return lax.dot_general(a, b, TN_DIMS, preferred_element_type=F32)


def _const_spec(shape):
    n = len(shape)
    return pl.BlockSpec(shape, lambda *_: (0,) * n, pipeline_mode=pl.Buffered(1))


def _params(n_axes):
    return pltpu.CompilerParams(
        dimension_semantics=("arbitrary",) * n_axes,
        vmem_limit_bytes=VMEM_LIMIT_BYTES)


def _ffn_kernel(x_ref, g_ref, win_ref, wout_ref, o_ref, act_ref):
    x = x_ref[...]
    h = _rms(x, g_ref[0:1, :]).astype(BF16)
    for c in range(D_FF // MXU_DIM):
        cols = slice(c * MXU_DIM, (c + 1) * MXU_DIM)
        ucols = slice(D_FF + c * MXU_DIM, D_FF + (c + 1) * MXU_DIM)
        a = _dot(h, win_ref[:, cols])
        u = _dot(h, win_ref[:, ucols])
        act_ref[:, cols] = (_silu(a) * u).astype(BF16)
    y = _dot(act_ref[...], wout_ref[...])
    o_ref[...] = x + 0.5 * _rms(y, g_ref[1:2, :])


def _ffn(x2d, g2, w_in, w_out):
    m = x2d.shape[0]
    tm = min(FFN_ROWS, m)
    return pl.pallas_call(
        _ffn_kernel,
        grid=(m // tm,),
        in_specs=[
            pl.BlockSpec((tm, D_MODEL), lambda i: (i, 0)),
            _const_spec((2, D_MODEL)),
            _const_spec((D_MODEL, 2 * D_FF)),
            _const_spec((D_FF, D_MODEL)),
        ],
        out_specs=pl.BlockSpec((tm, D_MODEL), lambda i: (i, 0)),
        out_shape=jax.ShapeDtypeStruct((m, D_MODEL), F32),
        scratch_shapes=[pltpu.VMEM((tm, D_FF), BF16)],
        compiler_params=_params(1),
        name="ffn",
    )(x2d, g2, w_in, w_out)


def _ret_kernel(*refs, t_tile, has_state):
    if has_state:
        (x_ref, g_ref, win_ref, wout_ref, gn_ref, rope_ref, dmat_ref, lg_ref, s0_ref,
         o_ref, s_ref, q_s, k_s, v_s, gate_s, mix_s) = refs
    else:
        (x_ref, g_ref, win_ref, wout_ref, gn_ref, rope_ref, dmat_ref, lg_ref,
         o_ref, s_ref, q_s, k_s, v_s, gate_s, mix_s) = refs
        s0_ref = None
    t = pl.program_id(1)

    @pl.when(t == 0)
    def _():
        if has_state:
            s_ref[...] = s0_ref[...]
        else:
            s_ref[...] = jnp.zeros_like(s_ref)

    x = x_ref[...]
    h = _rms(x, g_ref[0:1, :]).astype(BF16)
    q_s[...] = _dot(h, win_ref[:, 0:RET_QK])
    k_s[...] = _dot(h, win_ref[:, RET_QK:2 * RET_QK])
    v_s[...] = _dot(h, win_ref[:, 2 * RET_QK:2 * RET_QK + RET_V]).astype(BF16)
    gate_s[...] = _dot(h, win_ref[:, 2 * RET_QK + RET_V:])

    cq, sq, ck, sk = rope_ref[0], rope_ref[1], rope_ref[2], rope_ref[3]
    row = lax.broadcasted_iota(jnp.int32, (t_tile, LANES), 0).astype(F32)
    for hd in range(RET_H):
        qk = slice(hd * RET_DK, (hd + 1) * RET_DK)
        vv = slice(hd * RET_DV, (hd + 1) * RET_DV)
        lg = lg_ref[hd]
        lg_k = lg[:, :RET_DK]
        qh = q_s[:, qk]
        kh = k_s[:, qk]
        qr = qh * cq + pltpu.roll(qh, RET_DK // 2, 1) * sq
        kr = kh * ck + pltpu.roll(kh, RET_DK // 2, 1) * sk
        vh = v_s[:, vv]
        p = (_dot_nt(qr.astype(BF16), kr.astype(BF16)) * dmat_ref[hd]).astype(BF16)
        q_cross = (qr * jnp.exp(lg_k * (row + 1.0))).astype(BF16)
        k_dec = (kr * jnp.exp(lg_k * (float(t_tile - 1) - row))).astype(BF16)
        s_old = s_ref[hd]
        o = _dot(p, vh) + _dot(q_cross, s_old.astype(BF16))
        s_ref[hd] = jnp.exp(lg * float(t_tile)) * s_old + _dot_tn(k_dec, vh)
        o = _rms(o, gn_ref[:, vv])
        mix_s[:, vv] = (_silu(gate_s[:, vv]) * o).astype(BF16)
    y = _dot(mix_s[...], wout_ref[...])
    o_ref[...] = x + _rms(y, g_ref[1:2, :])


def _retention(x, g2, w_in, w_out, gn, rope_tab, dmat, lg_tab, s0, t_tile):
    b, l, _ = x.shape
    has_state = s0 is not None
    state_spec = pl.BlockSpec((None, RET_H, RET_DK, RET_DV), lambda i, j: (i, 0, 0, 0))
    in_specs = [
        pl.BlockSpec((None, t_tile, D_MODEL), lambda i, j: (i, j, 0)),
        _const_spec((2, D_MODEL)),
        _const_spec(w_in.shape),
        _const_spec(w_out.shape),
        _const_spec((1, RET_V)),
        pl.BlockSpec((4, t_tile, LANES), lambda i, j: (0, j, 0)),
        _const_spec(dmat.shape),
        _const_spec(lg_tab.shape),
    ]
    args = [x, g2, w_in, w_out, gn, rope_tab, dmat, lg_tab]
    if has_state:
        in_specs.append(state_spec)
        args.append(s0)
    return pl.pallas_call(
        functools.partial(_ret_kernel, t_tile=t_tile, has_state=has_state),
        grid=(b, l // t_tile),
        in_specs=in_specs,
        out_specs=[pl.BlockSpec((None, t_tile, D_MODEL), lambda i, j: (i, j, 0)), state_spec],
        out_shape=[jax.ShapeDtypeStruct(x.shape, F32),
                   jax.ShapeDtypeStruct((b, RET_H, RET_DK, RET_DV), F32)],
        scratch_shapes=[
            pltpu.VMEM((t_tile, RET_QK), F32),
            pltpu.VMEM((t_tile, RET_QK), F32),
            pltpu.VMEM((t_tile, RET_V), BF16),
            pltpu.VMEM((t_tile, RET_V), F32),
            pltpu.VMEM((t_tile, RET_V), BF16),
        ],
        compiler_params=_params(2),
        name="retention",
    )(*args)


def _swa_expand(a):
    lane = lax.broadcasted_iota(jnp.int32, (1, LANES), 1)
    lo = lane < SWA_DH
    ar = pltpu.roll(a, SWA_DH, 1)
    zero = jnp.zeros_like(a)
    return (jnp.where(lo, a, zero).astype(BF16), jnp.where(lo, zero, ar).astype(BF16),
            jnp.where(lo, ar, zero).astype(BF16), jnp.where(lo, zero, a).astype(BF16))


def _swa_prompt_kernel(x_ref, g_ref, win_ref, wout_ref, sink_ref, bias_ref,
                       o_ref, ko_ref, vo_ref, q_s, kx_s, vt_s, vprev_s, att_s, *, t_tile):
    t = pl.program_id(1)
    half = 2 * CHUNK
    n_keys = WINDOW + half
    pairs = SWA_G // 2

    @pl.when(t == 0)
    def _():
        kx_s[:, 0:WINDOW, :] = jnp.zeros((4, WINDOW, LANES), BF16)
        vprev_s[...] = jnp.zeros((4, LANES, WINDOW), BF16)

    x = x_ref[...]
    h = _rms(x, g_ref[0:1, :]).astype(BF16)
    q_s[...] = _dot(h, win_ref[:, 0:SWA_Q]) * (SWA_DH ** -0.5)
    kv = _dot(h, win_ref[:, SWA_Q:])
    k_new = kv[:, 0:SWA_KV]
    v_new = kv[:, SWA_KV:]
    ko_ref[...] = k_new[t_tile - WINDOW:, :]
    vo_ref[...] = v_new[t_tile - WINDOW:, :]
    for i, a in enumerate(_swa_expand(k_new)):
        kx_s[i, WINDOW:, :] = a
    v_t = v_new.T
    ones = jnp.ones((SWA_DH, t_tile), F32)
    for grp in range(SWA_HKV):
        vg = v_t[grp * SWA_DH:(grp + 1) * SWA_DH, :]
        vt_s[2 * grp] = jnp.concatenate([vg, ones], axis=0).astype(BF16)
        vt_s[2 * grp + 1] = jnp.concatenate([ones, vg], axis=0).astype(BF16)

    first = jnp.where(t == 0, 1, 0)
    for hc in range(t_tile // half):
        keys = slice(hc * half, hc * half + n_keys)
        qrows = slice(hc * half, (hc + 1) * half)
        bias = bias_ref[first] if hc == 0 else bias_ref[0]
        for grp in range(SWA_HKV):
            for pp in range(pairs // 2):
                pa = grp * pairs + 2 * pp
                qst = jnp.concatenate(
                    [q_s[qrows, pa * LANES:(pa + 1) * LANES],
                     q_s[qrows, (pa + 1) * LANES:(pa + 2) * LANES]], axis=0).astype(BF16)
                outs = []
                for par in range(2):
                    idx = grp * 2 + par
                    s = _dot_nt(kx_s[idx, keys, :], qst) + bias
                    snk = sink_ref[(grp * 2 + pp) * 2 + par]
                    m = jnp.maximum(jnp.max(s, axis=0, keepdims=True), snk)
                    e = jnp.exp(s - m).astype(BF16)
                    if hc == 0:
                        v_keys = jnp.concatenate([vprev_s[idx], vt_s[idx, :, 0:half]], axis=1)
                    else:
                        v_keys = vt_s[idx, :, hc * half - WINDOW:(hc + 1) * half]
                    ov = _dot(v_keys, e)
                    o_rows = slice(par * SWA_DH, (par + 1) * SWA_DH)
                    d_row = slice((1 - par) * SWA_DH, (1 - par) * SWA_DH + 1)
                    den = ov[d_row, :] + jnp.exp(snk - m)
                    outs.append((ov[o_rows, :] * (1.0 / den)).astype(BF16))
                for a in range(2):
                    lanes = slice(a * half, (a + 1) * half)
                    base = (pa + a) * LANES
                    att_s[base:base + SWA_DH, qrows] = outs[0][:, lanes]
                    att_s[base + SWA_DH:base + LANES, qrows] = outs[1][:, lanes]

    kx_s[:, 0:WINDOW, :] = kx_s[:, t_tile:t_tile + WINDOW, :]
    vprev_s[...] = vt_s[:, :, t_tile - WINDOW:t_tile]

    y = _dot_tn(att_s[...], wout_ref[...])
    o_ref[...] = x + _rms(y, g_ref[1:2, :])


def _swa_prompt(x, g2, w_in, w_out, sink, t_tile):
    b, l, _ = x.shape
    half = 2 * CHUNK
    n_keys = WINDOW + half
    s = sink.astype(F32).reshape(SWA_HKV, SWA_G // 4, 2, 2)
    s = jnp.transpose(s, (0, 1, 3, 2)).reshape(-1, 1, 2, 1)
    sink_tab = jnp.broadcast_to(s, (s.shape[0], 1, 2, half)).reshape(-1, 1, 2 * half)
    kc = jnp.arange(n_keys)[:, None] // CHUNK
    qc = (jnp.arange(2 * half)[None, :] % half) // CHUNK
    band = (kc >= qc) & (kc <= qc + WINDOW // CHUNK)
    started = jnp.arange(n_keys)[:, None] >= WINDOW
    bias = jnp.stack([jnp.where(band, 0.0, -jnp.inf),
                      jnp.where(band & started, 0.0, -jnp.inf)]).astype(F32)
    kv_out = pl.BlockSpec((None, WINDOW, SWA_KV), lambda i, j: (i, 0, 0))
    return pl.pallas_call(
        functools.partial(_swa_prompt_kernel, t_tile=t_tile),
        grid=(b, l // t_tile),
        in_specs=[
            pl.BlockSpec((None, t_tile, D_MODEL), lambda i, j: (i, j, 0)),
            _const_spec((2, D_MODEL)),
            _const_spec(w_in.shape),
            _const_spec(w_out.shape),
            _const_spec(sink_tab.shape),
            _const_spec(bias.shape),
        ],
        out_specs=[pl.BlockSpec((None, t_tile, D_MODEL), lambda i, j: (i, j, 0)), kv_out, kv_out],
        out_shape=[jax.ShapeDtypeStruct(x.shape, F32),
                   jax.ShapeDtypeStruct((b, WINDOW, SWA_KV), F32),
                   jax.ShapeDtypeStruct((b, WINDOW, SWA_KV), F32)],
        scratch_shapes=[
            pltpu.VMEM((t_tile, SWA_Q), F32),
            pltpu.VMEM((4, WINDOW + t_tile, LANES), BF16),
            pltpu.VMEM((4, LANES, t_tile), BF16),
            pltpu.VMEM((4, LANES, WINDOW), BF16),
            pltpu.VMEM((SWA_Q, t_tile), BF16),
        ],
        compiler_params=_params(2),
        name="swa_prompt",
    )(x, g2, w_in, w_out, sink_tab, bias)


def _swa_decode_kernel(x_ref, g_ref, win_ref, wout_ref, sink_ref, kc_ref, vc_ref,
                       o_ref, ko_ref, vo_ref, q_s, kx_s, vx_s, att_s, *, t_tile):
    pairs = SWA_G // 2
    for i, a in enumerate(_swa_expand(kc_ref[...])):
        kx_s[i, 0:WINDOW, :] = a
    for i, a in enumerate(_swa_expand(vc_ref[...])):
        vx_s[i, 0:WINDOW, :] = a

    x = x_ref[...]
    h = _rms(x, g_ref[0:1, :]).astype(BF16)
    q_s[...] = _dot(h, win_ref[:, 0:SWA_Q]) * (SWA_DH ** -0.5)
    kv = _dot(h, win_ref[:, SWA_Q:])
    k_new = kv[:, 0:SWA_KV]
    v_new = kv[:, SWA_KV:]
    ko_ref[...] = k_new
    vo_ref[...] = v_new
    for i, a in enumerate(_swa_expand(k_new)):
        kx_s[i, WINDOW:, :] = a
    for i, a in enumerate(_swa_expand(v_new)):
        vx_s[i, WINDOW:, :] = a

    for grp in range(SWA_HKV):
        qst = jnp.concatenate(
            [q_s[:, (grp * pairs + p) * LANES:(grp * pairs + p + 1) * LANES]
             for p in range(pairs)], axis=0).astype(BF16)
        out = None
        for par in range(2):
            idx = grp * 2 + par
            s = _dot_nt(qst, kx_s[idx])
            snk = sink_ref[idx][:, 0:1]
            m = jnp.maximum(jnp.max(s, axis=-1, keepdims=True), snk)
            e = jnp.exp(s - m)
            den = jnp.sum(e, axis=-1, keepdims=True) + jnp.exp(snk - m)
            pv = _dot((e / den).astype(BF16), vx_s[idx])
            out = pv if out is None else out + pv
        for p in range(pairs):
            cols = slice((grp * pairs + p) * LANES, (grp * pairs + p + 1) * LANES)
            att_s[:, cols] = out[p * t_tile:(p + 1) * t_tile, :].astype(BF16)

    y = _dot(att_s[...], wout_ref[...])
    o_ref[...] = x + _rms(y, g_ref[1:2, :])


def _swa_decode(x, g2, w_in, w_out, sink, k_cache, v_cache):
    b, l, _ = x.shape
    s = sink.astype(F32).reshape(SWA_HKV, SWA_G // 2, 2)
    s = jnp.transpose(s, (0, 2, 1)).reshape(SWA_HKV * 2, SWA_G // 2)
    s = jnp.repeat(s, l, axis=1)
    sink_tab = jnp.broadcast_to(s[:, :, None], s.shape + (LANES,))
    cache_spec = pl.BlockSpec((None, WINDOW, SWA_KV), lambda i: (i, 0, 0))
    kv_out = pl.BlockSpec((None, l, SWA_KV), lambda i: (i, 0, 0))
    return pl.pallas_call(
        functools.partial(_swa_decode_kernel, t_tile=l),
        grid=(b,),
        in_specs=[
            pl.BlockSpec((None, l, D_MODEL), lambda i: (i, 0, 0)),
            _const_spec((2, D_MODEL)),
            _const_spec(w_in.shape),
            _const_spec(w_out.shape),
            _const_spec(sink_tab.shape),
            cache_spec, cache_spec,
        ],
        out_specs=[pl.BlockSpec((None, l, D_MODEL), lambda i: (i, 0, 0)), kv_out, kv_out],
        out_shape=[jax.ShapeDtypeStruct(x.shape, F32),
                   jax.ShapeDtypeStruct((b, l, SWA_KV), F32),
                   jax.ShapeDtypeStruct((b, l, SWA_KV), F32)],
        scratch_shapes=[
            pltpu.VMEM((l, SWA_Q), F32),
            pltpu.VMEM((4, WINDOW + l, LANES), BF16),
            pltpu.VMEM((4, WINDOW + l, LANES), BF16),
            pltpu.VMEM((l, SWA_Q), BF16),
        ],
        compiler_params=_params(1),
        name="swa_decode",
    )(x, g2, w_in, w_out, sink_tab, k_cache, v_cache)


def _split3(a):
    hi = a.astype(BF16)
    r = a - hi.astype(F32)
    mid = r.astype(BF16)
    low = (r - mid.astype(F32)).astype(BF16)
    return hi, mid, low


def _hgrn_kernel(*refs, t_tile, has_state):
    if has_state:
        (x_ref, g_ref, win_ref, wout_ref, gn_ref, lb_ref, tri_ref, mask_ref, ones_ref, s0_ref,
         o_ref, s_ref, st_s, q_s, k_s, kp_s, fp_s, vp_s, vb_s, b_s, gate_s,
         qc1_s, kc1_s, qc2_s, kc2_s, mix_s) = refs
    else:
        (x_ref, g_ref, win_ref, wout_ref, gn_ref, lb_ref, tri_ref, mask_ref, ones_ref,
         o_ref, s_ref, st_s, q_s, k_s, kp_s, fp_s, vp_s, vb_s, b_s, gate_s,
         qc1_s, kc1_s, qc2_s, kc2_s, mix_s) = refs
    t = pl.program_id(1)
    pad = HG_SUB
    chunk = min(t_tile, HG_CHUNK)
    n1 = t_tile // chunk
    nb2 = chunk // HG_SUB
    body = slice(pad, pad + t_tile)

    @pl.when(t == 0)
    def _():
        for hd in range(HG_H):
            if has_state:
                st_s[hd] = s0_ref[hd].T
            else:
                st_s[hd] = jnp.zeros((HG_DV, HG_DK), F32)
        zpad = jnp.zeros((HG_H, pad, LANES), F32)
        kp_s[:, 0:pad, :] = zpad
        fp_s[:, 0:pad, :] = zpad
        vp_s[:, 0:pad, :] = zpad
        qc1_s[...] = jnp.zeros_like(qc1_s)
        kc1_s[...] = jnp.zeros_like(kc1_s)
        qc2_s[...] = jnp.zeros_like(qc2_s)
        kc2_s[...] = jnp.zeros_like(kc2_s)

    x = x_ref[...]
    h = _rms(x, g_ref[0:1, :]).astype(BF16)
    lb = lb_ref[...]
    q_s[...] = _silu(_dot(h, win_ref[:, 0:HG_QK]))
    f = _dot(h, win_ref[:, HG_QK:2 * HG_QK])
    fgate = lb + (1.0 - lb) * jax.nn.sigmoid(f)
    k_all = (1.0 - lb) * jax.nn.sigmoid(-f)
    k_s[...] = k_all
    v = _dot(h, win_ref[:, 2 * HG_QK:2 * HG_QK + HG_V])
    vb_s[...] = v.astype(BF16)
    for hd in range(HG_H):
        ln = slice(hd * LANES, (hd + 1) * LANES)
        fp_s[hd, body, :] = fgate[:, ln]
        kp_s[hd, body, :] = k_all[:, ln]
        vp_s[hd, body, :] = v[:, ln]
    gate_s[...] = _dot(h, win_ref[:, 2 * HG_QK + HG_V:])
    tri = tri_ref[...]
    hi, mid, low = _split3(jnp.log(fgate))
    b_s[...] = _dot(tri, hi) + _dot(tri, mid) + _dot(tri, low)
    ones2 = ones_ref[...]

    for hd in range(HG_H):
        ln = slice(hd * LANES, (hd + 1) * LANES)
        q = q_s[:, ln]
        k = k_s[:, ln]
        bc = b_s[:, ln]
        vb = vb_s[:, ln]

        for j in range(n1 - 1):
            blk = slice(j * chunk, (j + 1) * chunk)
            after = slice((j + 1) * chunk, t_tile)
            cols = slice(j * LANES, (j + 1) * LANES)
            e_j = bc[(j + 1) * chunk - 1:(j + 1) * chunk, :]
            kc1_s[blk, cols] = (k[blk] * jnp.exp(e_j - bc[blk])).astype(BF16)
            qc1_s[after, cols] = (q[after] * jnp.exp(bc[after] - e_j)).astype(BF16)
        for c in range(n1):
            for j in range(nb2 - 1):
                r0 = c * chunk + j * HG_SUB
                blk = slice(r0, r0 + HG_SUB)
                after = slice(r0 + HG_SUB, (c + 1) * chunk)
                cols = slice(j * LANES, (j + 1) * LANES)
                e_j = bc[r0 + HG_SUB - 1:r0 + HG_SUB, :]
                kc2_s[blk, cols] = k[blk] * jnp.exp(e_j - bc[blk])
                qc2_s[after, cols] = q[after] * jnp.exp(bc[after] - e_j)

        p = _dot_nt(qc2_s[...].astype(BF16), kc2_s[...].astype(BF16)) * mask_ref[1]
        if n1 > 1:
            p = p + _dot_nt(qc1_s[...], kc1_s[...]) * mask_ref[0]
        st_old = st_s[hd]
        o = _dot(p.astype(BF16), vb) + _dot_nt((q * jnp.exp(bc)).astype(BF16), st_old.astype(BF16))

        prod = None
        ws = []
        for d in range(HG_SUB):
            w = q * kp_s[hd, pad - d:pad - d + t_tile, :]
            if d >= 1:
                fsh = fp_s[hd, pad - d + 1:pad - d + 1 + t_tile, :]
                prod = fsh if d == 1 else prod * fsh
                w = w * prod
            ws.append(w.astype(BF16))
            if d % 2 == 1:
                r = _dot(jnp.concatenate(ws, axis=1), ones2)
                o = o + r[:, 0:LANES] * vp_s[hd, pad - d + 1:pad - d + 1 + t_tile, :]
                o = o + r[:, LANES:] * vp_s[hd, pad - d:pad - d + t_tile, :]
                ws = []

        b_last = bc[t_tile - 1:t_tile, :]
        k_end = (k * jnp.exp(b_last - bc)).astype(BF16)
        st_new = st_old * jnp.exp(b_last) + _dot_tn(vb, k_end)
        st_s[hd] = st_new
        s_ref[hd] = st_new.T

        o = _rms(o, gn_ref[:, ln])
        mix_s[:, ln] = (_silu(gate_s[:, ln]) * o).astype(BF16)

    y = _dot(mix_s[...], wout_ref[...])
    o_ref[...] = x + _rms(y, g_ref[1:2, :])


def _hgrn(x, g2, w_in, w_out, gn, lb, s0, t_tile):
    b, l, _ = x.shape
    has_state = s0 is not None
    chunk = min(t_tile, HG_CHUNK)
    n1 = t_tile // chunk
    nb2 = chunk // HG_SUB
    tri = jnp.tril(jnp.ones((t_tile, t_tile), F32)).astype(BF16)
    n = jnp.arange(t_tile)
    far = (n[:, None] - n[None, :]) >= HG_SUB
    same = (n[:, None] // chunk) == (n[None, :] // chunk)
    masks = jnp.stack([far, far & same]).astype(F32)
    blk = jnp.arange(2 * LANES) // LANES
    ones2 = (blk[:, None] == blk[None, :]).astype(BF16)
    state_spec = pl.BlockSpec((None, HG_H, HG_DK, HG_DV), lambda i, j: (i, 0, 0, 0))
    in_specs = [
        pl.BlockSpec((None, t_tile, D_MODEL), lambda i, j: (i, j, 0)),
        _const_spec((2, D_MODEL)),
        _const_spec(w_in.shape),
        _const_spec(w_out.shape),
        _const_spec((1, HG_V)),
        _const_spec((1, HG_QK)),
        _const_spec(tri.shape),
        _const_spec(masks.shape),
        _const_spec(ones2.shape),
    ]
    args = [x, g2, w_in, w_out, gn, lb, tri, masks, ones2]
    if has_state:
        in_specs.append(state_spec)
        args.append(s0)
    cols1 = max(n1 - 1, 1) * LANES
    cols2 = (nb2 - 1) * LANES
    return pl.pallas_call(
        functools.partial(_hgrn_kernel, t_tile=t_tile, has_state=has_state),
        grid=(b, l // t_tile),
        in_specs=in_specs,
        out_specs=[pl.BlockSpec((None, t_tile, D_MODEL), lambda i, j: (i, j, 0)), state_spec],
        out_shape=[jax.ShapeDtypeStruct(x.shape, F32),
                   jax.ShapeDtypeStruct((b, HG_H, HG_DK, HG_DV), F32)],
        scratch_shapes=[
            pltpu.VMEM((HG_H, HG_DV, HG_DK), F32),
            pltpu.VMEM((t_tile, HG_QK), F32),
            pltpu.VMEM((t_tile, HG_QK), F32),
            pltpu.VMEM((HG_H, HG_SUB + t_tile, LANES), F32),
            pltpu.VMEM((HG_H, HG_SUB + t_tile, LANES), F32),
            pltpu.VMEM((HG_H, HG_SUB + t_tile, LANES), F32),
            pltpu.VMEM((t_tile, HG_V), BF16),
            pltpu.VMEM((t_tile, HG_QK), F32),
            pltpu.VMEM((t_tile, HG_V), F32),
            pltpu.VMEM((t_tile, cols1), BF16),
            pltpu.VMEM((t_tile, cols1), BF16),
            pltpu.VMEM((t_tile, cols2), F32),
            pltpu.VMEM((t_tile, cols2), F32),
            pltpu.VMEM((t_tile, HG_V), BF16),
        ],
        compiler_params=_params(2),
        name="hgrn2",
    )(*args)


def _rope_table(pos0, length):
    half = RET_DK // 2
    inv = ROPE_BASE ** (-jnp.arange(half, dtype=F32) / half)
    pos = pos0 + jnp.arange(length, dtype=F32)
    ang = pos[:, None] * inv[None, :]
    cos = jnp.cos(ang)
    sin = jnp.sin(ang)
    c2 = jnp.concatenate([cos, cos], axis=-1)
    s2 = jnp.concatenate([-sin, sin], axis=-1)
    ks = RET_DK ** -0.5
    return jnp.stack([c2, s2, c2 * ks, s2 * ks])


def _ret_log_decay():
    return jnp.log1p(-(2.0 ** (-5.0 - jnp.arange(RET_H, dtype=F32))))


def _ret_decay_matrix(t_tile, chunk):
    lg = _ret_log_decay()
    n = jnp.arange(t_tile)
    cn = n // chunk
    diff = (n[:, None] - n[None, :]).astype(F32)
    same = cn[:, None] == cn[None, :]
    later = cn[:, None] > cn[None, :]
    expo = jnp.where(same, jnp.abs(diff), diff)
    dec = jnp.exp(lg[:, None, None] * expo[None])
    return jnp.where((same | later)[None], dec, 0.0)


def _lower_bounds(lb_param):
    c = jnp.cumsum(jax.nn.softmax(lb_param.astype(F32), axis=0), axis=0)
    return c - c[0]


def _trunk(x, pos0, chunk, t_tile, state_ret, cache_k, cache_v, state_hgrn, params):
    (norm_g, w_ff_in, w_ff_out, ret_w_in, ret_w_out, ret_gn_g, swa_w_in, swa_w_out, swa_sink,
     hg_w_in, hg_w_out, hg_gn_g, lbs) = params
    b, l, _ = x.shape
    has_past = state_ret is not None
    rope_tab = _rope_table(pos0, l)
    dmat = _ret_decay_matrix(t_tile, chunk)
    lg_tab = jnp.broadcast_to(_ret_log_decay()[:, None, None], (RET_H, 1, RET_DV))
    ret_out, k_out, v_out, hg_out = [], [], [], []

    def ffn(x, li, which):
        g2 = norm_g[li, 4 * which:4 * which + 2]
        y = _ffn(x.reshape(b * l, D_MODEL), g2, w_ff_in[li, which], w_ff_out[li, which])
        return y.reshape(b, l, D_MODEL)

    for li in range(DEPTH):
        kind, j = li % 3, li // 3
        x = ffn(x, li, 0)
        g2 = norm_g[li, 2:4]
        if kind == 0:
            x, s_new = _retention(x, g2, ret_w_in[j], ret_w_out[j], ret_gn_g[j][None, :], rope_tab,
                                  dmat, lg_tab, state_ret[j] if has_past else None, t_tile)
            ret_out.append(s_new)
        elif kind == 1:
            if has_past:
                kc = cache_k[j].reshape(b, WINDOW, SWA_KV)
                vc = cache_v[j].reshape(b, WINDOW, SWA_KV)
                x, kn, vn = _swa_decode(x, g2, swa_w_in[j], swa_w_out[j], swa_sink[j], kc, vc)
            else:
                x, kn, vn = _swa_prompt(x, g2, swa_w_in[j], swa_w_out[j], swa_sink[j], t_tile)
            k_out.append(kn.reshape(b, -1, SWA_HKV, SWA_DH))
            v_out.append(vn.reshape(b, -1, SWA_HKV, SWA_DH))
        else:
            x, s_new = _hgrn(x, g2, hg_w_in[j], hg_w_out[j], hg_gn_g[j][None, :], lbs[li][None, :],
                             state_hgrn[j] if has_past else None, t_tile)
            hg_out.append(s_new)
        x = ffn(x, li, 1)
    return x, jnp.stack(ret_out), jnp.stack(k_out), jnp.stack(v_out), jnp.stack(hg_out)


def kernel(x_prompt, x_sample, state_ret, cache_swa_k, cache_swa_v, state_hgrn, norm_g, w_ff_in, w_ff_out,
           ret_w_in, ret_w_out, ret_gn_g, swa_w_in, swa_w_out, swa_sink, hg_w_in, hg_w_out, hg_gn_g, hg_lb):
    bf = lambda w: w.astype(BF16)
    params = (norm_g, bf(w_ff_in), bf(w_ff_out), bf(ret_w_in), bf(ret_w_out), ret_gn_g,
              bf(swa_w_in), bf(swa_w_out), swa_sink, bf(hg_w_in), bf(hg_w_out), hg_gn_g,
              _lower_bounds(hg_lb))
    y_p, ret_p, k_p, v_p, hg_p = _trunk(x_prompt, 0.0, CHUNK, MIX_TOKENS, None, None, None, None, params)
    dec_len = x_sample.shape[1]
    y_s, ret_s, k_s, v_s, hg_s = _trunk(x_sample, float(PAST_LEN), dec_len, dec_len, state_ret,
                                        cache_swa_k, cache_swa_v, state_hgrn, params)
    return (y_p, y_s, ret_p, ret_s, k_p, v_p, k_s, v_s, hg_p, hg_s)
```

```python
import functools

import jax
import jax.numpy as jnp
from jax import lax
from jax.experimental import pallas as pl
from jax.experimental.pallas import tpu as pltpu

F32 = jnp.float32
BF16 = jnp.bfloat16

D_MODEL = 1024
DEPTH = 4
CHUNK = 64
PAST_LEN = 4096
D_FF = 2816
EPS = 1e-6
RET_H, RET_DK, RET_DV = 8, 128, 256
RET_QK = RET_H * RET_DK
RET_V = RET_H * RET_DV
ROPE_BASE = 10000.0
WINDOW = 128
SWA_HQ, SWA_HKV, SWA_DH = 16, 2, 64
SWA_G = SWA_HQ // SWA_HKV
SWA_Q = SWA_HQ * SWA_DH
SWA_KV = SWA_HKV * SWA_DH
HG_H, HG_DK, HG_DV = 8, 128, 128
HG_QK = HG_H * HG_DK
HG_V = HG_H * HG_DV

VMEM_LIMIT_BYTES = 56 * 1024 * 1024
LANES = 128
SUBLANES = 8
MXU_DIM = 256
FFN_ROWS = 512
MIX_TOKENS = 256
HG_CHUNK = 64
HG_SUB = SUBLANES

NT_DIMS = (((1,), (1,)), ((), ()))
TN_DIMS = (((0,), (0,)), ((), ()))


def _rms(x, g):
    return x * lax.rsqrt(jnp.mean(x * x, axis=-1, keepdims=True) + EPS) * g


def _silu(x):
    return x * jax.nn.sigmoid(x)


def _dot(a, b):
    return jnp.dot(a, b, preferred_element_type=F32)


def _dot_nt(a, b):
    return lax.dot_general(a, b, NT_DIMS, preferred_element_type=F32)


def _dot_tn(a, b):
    return lax.dot_general(a, b, TN_DIMS, preferred_element_type=F32)


def _const_spec(shape):
    n = len(shape)
    return pl.BlockSpec(shape, lambda *_: (0,) * n, pipeline_mode=pl.Buffered(1))


def _layer_spec(stacked, idx):
    tail = stacked.shape[len(idx):]
    block = (None,) * len(idx) + tail
    return pl.BlockSpec(block, lambda *_: tuple(idx) + (0,) * len(tail), pipeline_mode=pl.Buffered(1))


def _params(n_axes):
    return pltpu.CompilerParams(
        dimension_semantics=("arbitrary",) * n_axes,
        vmem_limit_bytes=VMEM_LIMIT_BYTES)


def _ffn_kernel(xp_ref, xs_ref, g_ref, win_ref, wout_ref, op_ref, os_ref, act_ref):
    i = pl.program_id(0)
    n_prompt = pl.num_programs(0) - 1
    x = jnp.where(i == n_prompt, xs_ref[...], xp_ref[...])
    h = _rms(x, g_ref[0:1, :]).astype(BF16)
    for c in range(D_FF // MXU_DIM):
        cols = slice(c * MXU_DIM, (c + 1) * MXU_DIM)
        ucols = slice(D_FF + c * MXU_DIM, D_FF + (c + 1) * MXU_DIM)
        a = _dot(h, win_ref[:, cols])
        u = _dot(h, win_ref[:, ucols])
        act_ref[:, cols] = (_silu(a) * u).astype(BF16)
    y = _dot(act_ref[...], wout_ref[...])
    out = x + 0.5 * _rms(y, g_ref[1:2, :])
    os_ref[...] = out

    @pl.when(i < n_prompt)
    def _():
        op_ref[...] = out


def _ffn(xp, xs, g2, w_in, w_out, widx):
    m = xp.shape[0]
    tm = FFN_ROWS
    assert m % tm == 0 and xs.shape[0] == tm
    n_prompt = m // tm
    prompt_spec = pl.BlockSpec((tm, D_MODEL), lambda i: (jnp.minimum(i, n_prompt - 1), 0))
    sample_spec = pl.BlockSpec((tm, D_MODEL), lambda i: (0, 0))
    return pl.pallas_call(
        _ffn_kernel,
        grid=(n_prompt + 1,),
        in_specs=[prompt_spec, sample_spec, _const_spec((2, D_MODEL)),
                  _layer_spec(w_in, widx), _layer_spec(w_out, widx)],
        out_specs=[prompt_spec, sample_spec],
        out_shape=[jax.ShapeDtypeStruct(xp.shape, F32), jax.ShapeDtypeStruct(xs.shape, F32)],
        scratch_shapes=[pltpu.VMEM((tm, D_FF), BF16)],
        compiler_params=_params(1),
        name="ffn",
    )(xp, xs, g2, w_in, w_out)


def _ret_kernel(*refs, t_tile, has_state):
    if has_state:
        (x_ref, g_ref, win_ref, wout_ref, gn_ref, rope_ref, dmat_ref, lg_ref, s0_ref,
         o_ref, s_ref, q_s, k_s, v_s, gate_s, mix_s) = refs
    else:
        (x_ref, g_ref, win_ref, wout_ref, gn_ref, rope_ref, dmat_ref, lg_ref,
         o_ref, s_ref, q_s, k_s, v_s, gate_s, mix_s) = refs
        s0_ref = None
    t = pl.program_id(1)

    @pl.when(t == 0)
    def _():
        if has_state:
            s_ref[...] = s0_ref[...]
        else:
            s_ref[...] = jnp.zeros_like(s_ref)

    x = x_ref[...]
    h = _rms(x, g_ref[0:1, :]).astype(BF16)
    q_s[...] = _dot(h, win_ref[:, 0:RET_QK])
    k_s[...] = _dot(h, win_ref[:, RET_QK:2 * RET_QK])
    v_s[...] = _dot(h, win_ref[:, 2 * RET_QK:2 * RET_QK + RET_V]).astype(BF16)
    gate_s[...] = _dot(h, win_ref[:, 2 * RET_QK + RET_V:])

    cq, sq, ck, sk = rope_ref[0], rope_ref[1], rope_ref[2], rope_ref[3]
    row = lax.broadcasted_iota(jnp.int32, (t_tile, LANES), 0).astype(F32)
    for hd in range(RET_H):
        qk = slice(hd * RET_DK, (hd + 1) * RET_DK)
        vv = slice(hd * RET_DV, (hd + 1) * RET_DV)
        lg = lg_ref[hd]
        lg_k = lg[:, :RET_DK]
        qh = q_s[:, qk]
        kh = k_s[:, qk]
        qr = qh * cq + pltpu.roll(qh, RET_DK // 2, 1) * sq
        kr = kh * ck + pltpu.roll(kh, RET_DK // 2, 1) * sk
        vh = v_s[:, vv]
        p = (_dot_nt(qr.astype(BF16), kr.astype(BF16)) * dmat_ref[hd]).astype(BF16)
        q_cross = (qr * jnp.exp(lg_k * (row + 1.0))).astype(BF16)
        k_dec = (kr * jnp.exp(lg_k * (float(t_tile - 1) - row))).astype(BF16)
        s_old = s_ref[hd]
        o = _dot(p, vh) + _dot(q_cross, s_old.astype(BF16))
        s_ref[hd] = jnp.exp(lg * float(t_tile)) * s_old + _dot_tn(k_dec, vh)
        o = _rms(o, gn_ref[:, vv])
        mix_s[:, vv] = (_silu(gate_s[:, vv]) * o).astype(BF16)
    y = _dot(mix_s[...], wout_ref[...])
    o_ref[...] = x + _rms(y, g_ref[1:2, :])


def _retention(x, g2, w_in, w_out, widx, gn, rope_tab, dmat, lg_tab, s0, t_tile):
    b, l, _ = x.shape
    has_state = s0 is not None
    state_spec = pl.BlockSpec((None, RET_H, RET_DK, RET_DV), lambda i, j: (i, 0, 0, 0))
    in_specs = [
        pl.BlockSpec((None, t_tile, D_MODEL), lambda i, j: (i, j, 0)),
        _const_spec((2, D_MODEL)),
        _layer_spec(w_in, widx),
        _layer_spec(w_out, widx),
        _const_spec((1, RET_V)),
        pl.BlockSpec((4, t_tile, LANES), lambda i, j: (0, j, 0)),
        _const_spec(dmat.shape),
        _const_spec(lg_tab.shape),
    ]
    args = [x, g2, w_in, w_out, gn, rope_tab, dmat, lg_tab]
    if has_state:
        in_specs.append(state_spec)
        args.append(s0)
    return pl.pallas_call(
        functools.partial(_ret_kernel, t_tile=t_tile, has_state=has_state),
        grid=(b, l // t_tile),
        in_specs=in_specs,
        out_specs=[pl.BlockSpec((None, t_tile, D_MODEL), lambda i, j: (i, j, 0)), state_spec],
        out_shape=[jax.ShapeDtypeStruct(x.shape, F32),
                   jax.ShapeDtypeStruct((b, RET_H, RET_DK, RET_DV), F32)],
        scratch_shapes=[
            pltpu.VMEM((t_tile, RET_QK), F32),
            pltpu.VMEM((t_tile, RET_QK), F32),
            pltpu.VMEM((t_tile, RET_V), BF16),
            pltpu.VMEM((t_tile, RET_V), F32),
            pltpu.VMEM((t_tile, RET_V), BF16),
        ],
        compiler_params=_params(2),
        name="retention",
    )(*args)


def _swa_expand(a):
    lane = lax.broadcasted_iota(jnp.int32, (1, LANES), 1)
    lo = lane < SWA_DH
    ar = pltpu.roll(a, SWA_DH, 1)
    zero = jnp.zeros_like(a)
    return (jnp.where(lo, a, zero).astype(BF16), jnp.where(lo, zero, ar).astype(BF16),
            jnp.where(lo, ar, zero).astype(BF16), jnp.where(lo, zero, a).astype(BF16))


def _swa_prompt_kernel(x_ref, g_ref, win_ref, wout_ref, sink_ref, bias_ref,
                       o_ref, ko_ref, vo_ref, q_s, kx_s, vt_s, vprev_s, att_s, sc_s, ex_s, *, t_tile):
    t = pl.program_id(1)
    half = 2 * CHUNK
    n_keys = WINDOW + half
    pairs = SWA_G // 2

    @pl.when(t == 0)
    def _():
        kx_s[:, 0:WINDOW, :] = jnp.zeros((4, WINDOW, LANES), BF16)
        vprev_s[...] = jnp.zeros((4, LANES, WINDOW), BF16)

    x = x_ref[...]
    h = _rms(x, g_ref[0:1, :]).astype(BF16)
    q_s[...] = _dot(h, win_ref[:, 0:SWA_Q]) * (SWA_DH ** -0.5)
    kv = _dot(h, win_ref[:, SWA_Q:])
    k_new = kv[:, 0:SWA_KV]
    v_new = kv[:, SWA_KV:]
    ko_ref[...] = k_new[t_tile - WINDOW:, :]
    vo_ref[...] = v_new[t_tile - WINDOW:, :]
    for i, a in enumerate(_swa_expand(k_new)):
        kx_s[i, WINDOW:, :] = a
    v_t = v_new.T
    ones = jnp.ones((SWA_DH, t_tile), F32)
    for grp in range(SWA_HKV):
        vg = v_t[grp * SWA_DH:(grp + 1) * SWA_DH, :]
        vt_s[2 * grp] = jnp.concatenate([vg, ones], axis=0).astype(BF16)
        vt_s[2 * grp + 1] = jnp.concatenate([ones, vg], axis=0).astype(BF16)

    first = jnp.where(t == 0, 1, 0)
    combos = [(hc, grp, pp, par) for hc in range(t_tile // half) for grp in range(SWA_HKV)
              for pp in range(pairs // 2) for par in range(2)]
    for i, (hc, grp, pp, par) in enumerate(combos):
        keys = slice(hc * half, hc * half + n_keys)
        qrows = slice(hc * half, (hc + 1) * half)
        bias = bias_ref[first] if hc == 0 else bias_ref[0]
        pa = grp * pairs + 2 * pp
        qst = jnp.concatenate(
            [q_s[qrows, pa * LANES:(pa + 1) * LANES],
             q_s[qrows, (pa + 1) * LANES:(pa + 2) * LANES]], axis=0).astype(BF16)
        sc_s[i] = _dot_nt(kx_s[grp * 2 + par, keys, :], qst) + bias
    sink_terms = []
    for i, (hc, grp, pp, par) in enumerate(combos):
        s = sc_s[i]
        snk = sink_ref[(grp * 2 + pp) * 2 + par]
        m = jnp.maximum(jnp.max(s, axis=0, keepdims=True), snk)
        ex_s[i] = jnp.exp(s - m).astype(BF16)
        sink_terms.append(jnp.exp(snk - m))
    for i, (hc, grp, pp, par) in enumerate(combos):
        idx = grp * 2 + par
        qrows = slice(hc * half, (hc + 1) * half)
        if hc == 0:
            v_keys = jnp.concatenate([vprev_s[idx], vt_s[idx, :, 0:half]], axis=1)
        else:
            v_keys = vt_s[idx, :, hc * half - WINDOW:(hc + 1) * half]
        ov = _dot(v_keys, ex_s[i])
        o_rows = slice(par * SWA_DH, (par + 1) * SWA_DH)
        d_row = slice((1 - par) * SWA_DH, (1 - par) * SWA_DH + 1)
        out = (ov[o_rows, :] * (1.0 / (ov[d_row, :] + sink_terms[i]))).astype(BF16)
        pa = grp * pairs + 2 * pp
        for a in range(2):
            base = (pa + a) * LANES + par * SWA_DH
            att_s[base:base + SWA_DH, qrows] = out[:, a * half:(a + 1) * half]

    kx_s[:, 0:WINDOW, :] = kx_s[:, t_tile:t_tile + WINDOW, :]
    vprev_s[...] = vt_s[:, :, t_tile - WINDOW:t_tile]

    y = _dot_tn(att_s[...], wout_ref[...])
    o_ref[...] = x + _rms(y, g_ref[1:2, :])


def _swa_prompt(x, g2, w_in, w_out, widx, sink, t_tile):
    b, l, _ = x.shape
    half = 2 * CHUNK
    n_keys = WINDOW + half
    s = sink.astype(F32).reshape(SWA_HKV, SWA_G // 4, 2, 2)
    s = jnp.transpose(s, (0, 1, 3, 2)).reshape(-1, 1, 2, 1)
    sink_tab = jnp.broadcast_to(s, (s.shape[0], 1, 2, half)).reshape(-1, 1, 2 * half)
    kc = jnp.arange(n_keys)[:, None] // CHUNK
    qc = (jnp.arange(2 * half)[None, :] % half) // CHUNK
    band = (kc >= qc) & (kc <= qc + WINDOW // CHUNK)
    started = jnp.arange(n_keys)[:, None] >= WINDOW
    bias = jnp.stack([jnp.where(band, 0.0, -jnp.inf),
                      jnp.where(band & started, 0.0, -jnp.inf)]).astype(F32)
    kv_out = pl.BlockSpec((None, WINDOW, SWA_KV), lambda i, j: (i, 0, 0))
    return pl.pallas_call(
        functools.partial(_swa_prompt_kernel, t_tile=t_tile),
        grid=(b, l // t_tile),
        in_specs=[
            pl.BlockSpec((None, t_tile, D_MODEL), lambda i, j: (i, j, 0)),
            _const_spec((2, D_MODEL)),
            _layer_spec(w_in, widx),
            _layer_spec(w_out, widx),
            _const_spec(sink_tab.shape),
            _const_spec(bias.shape),
        ],
        out_specs=[pl.BlockSpec((None, t_tile, D_MODEL), lambda i, j: (i, j, 0)), kv_out, kv_out],
        out_shape=[jax.ShapeDtypeStruct(x.shape, F32),
                   jax.ShapeDtypeStruct((b, WINDOW, SWA_KV), F32),
                   jax.ShapeDtypeStruct((b, WINDOW, SWA_KV), F32)],
        scratch_shapes=[
            pltpu.VMEM((t_tile, SWA_Q), F32),
            pltpu.VMEM((4, WINDOW + t_tile, LANES), BF16),
            pltpu.VMEM((4, LANES, t_tile), BF16),
            pltpu.VMEM((4, LANES, WINDOW), BF16),
            pltpu.VMEM((SWA_Q, t_tile), BF16),
            pltpu.VMEM((4 * t_tile // CHUNK, n_keys, 2 * half), F32),
            pltpu.VMEM((4 * t_tile // CHUNK, n_keys, 2 * half), BF16),
        ],
        compiler_params=_params(2),
        name="swa_prompt",
    )(x, g2, w_in, w_out, sink_tab, bias)


def _swa_decode_kernel(x_ref, g_ref, win_ref, wout_ref, sink_ref, kc_ref, vc_ref,
                       o_ref, ko_ref, vo_ref, q_s, kx_s, vx_s, att_s, *, t_tile):
    pairs = SWA_G // 2
    for i, a in enumerate(_swa_expand(kc_ref[...])):
        kx_s[i, 0:WINDOW, :] = a
    for i, a in enumerate(_swa_expand(vc_ref[...])):
        vx_s[i, 0:WINDOW, :] = a

    x = x_ref[...]
    h = _rms(x, g_ref[0:1, :]).astype(BF16)
    q_s[...] = _dot(h, win_ref[:, 0:SWA_Q]) * (SWA_DH ** -0.5)
    kv = _dot(h, win_ref[:, SWA_Q:])
    k_new = kv[:, 0:SWA_KV]
    v_new = kv[:, SWA_KV:]
    ko_ref[...] = k_new
    vo_ref[...] = v_new
    for i, a in enumerate(_swa_expand(k_new)):
        kx_s[i, WINDOW:, :] = a
    for i, a in enumerate(_swa_expand(v_new)):
        vx_s[i, WINDOW:, :] = a

    for grp in range(SWA_HKV):
        qst = jnp.concatenate(
            [q_s[:, (grp * pairs + p) * LANES:(grp * pairs + p + 1) * LANES]
             for p in range(pairs)], axis=0).astype(BF16)
        out = None
        for par in range(2):
            idx = grp * 2 + par
            s = _dot_nt(qst, kx_s[idx])
            snk = sink_ref[idx][:, 0:1]
            m = jnp.maximum(jnp.max(s, axis=-1, keepdims=True), snk)
            e = jnp.exp(s - m)
            den = jnp.sum(e, axis=-1, keepdims=True) + jnp.exp(snk - m)
            pv = _dot((e / den).astype(BF16), vx_s[idx])
            out = pv if out is None else out + pv
        for p in range(pairs):
            cols = slice((grp * pairs + p) * LANES, (grp * pairs + p + 1) * LANES)
            att_s[:, cols] = out[p * t_tile:(p + 1) * t_tile, :].astype(BF16)

    y = _dot(att_s[...], wout_ref[...])
    o_ref[...] = x + _rms(y, g_ref[1:2, :])


def _swa_decode(x, g2, w_in, w_out, widx, sink, k_cache, v_cache):
    b, l, _ = x.shape
    s = sink.astype(F32).reshape(SWA_HKV, SWA_G // 2, 2)
    s = jnp.transpose(s, (0, 2, 1)).reshape(SWA_HKV * 2, SWA_G // 2)
    s = jnp.repeat(s, l, axis=1)
    sink_tab = jnp.broadcast_to(s[:, :, None], s.shape + (LANES,))
    cache_spec = pl.BlockSpec((None, WINDOW, SWA_KV), lambda i: (i, 0, 0))
    kv_out = pl.BlockSpec((None, l, SWA_KV), lambda i: (i, 0, 0))
    return pl.pallas_call(
        functools.partial(_swa_decode_kernel, t_tile=l),
        grid=(b,),
        in_specs=[
            pl.BlockSpec((None, l, D_MODEL), lambda i: (i, 0, 0)),
            _const_spec((2, D_MODEL)),
            _layer_spec(w_in, widx),
            _layer_spec(w_out, widx),
            _const_spec(sink_tab.shape),
            cache_spec, cache_spec,
        ],
        out_specs=[pl.BlockSpec((None, l, D_MODEL), lambda i: (i, 0, 0)), kv_out, kv_out],
        out_shape=[jax.ShapeDtypeStruct(x.shape, F32),
                   jax.ShapeDtypeStruct((b, l, SWA_KV), F32),
                   jax.ShapeDtypeStruct((b, l, SWA_KV), F32)],
        scratch_shapes=[
            pltpu.VMEM((l, SWA_Q), F32),
            pltpu.VMEM((4, WINDOW + l, LANES), BF16),
            pltpu.VMEM((4, WINDOW + l, LANES), BF16),
            pltpu.VMEM((l, SWA_Q), BF16),
        ],
        compiler_params=_params(1),
        name="swa_decode",
    )(x, g2, w_in, w_out, sink_tab, k_cache, v_cache)


def _split3(a):
    hi = a.astype(BF16)
    r = a - hi.astype(F32)
    mid = r.astype(BF16)
    low = (r - mid.astype(F32)).astype(BF16)
    return hi, mid, low


def _hgrn_kernel(*refs, t_tile, has_state):
    if has_state:
        (x_ref, g_ref, win_ref, wout_ref, gn_ref, lb_ref, tri_ref, mask_ref, ones_ref, s0_ref,
         o_ref, s_ref, st_s, q_s, k_s, kp_s, fp_s, vp_s, vb_s, b_s, gate_s,
         qc1_s, kc1_s, qc2_s, kc2_s, mix_s) = refs
    else:
        (x_ref, g_ref, win_ref, wout_ref, gn_ref, lb_ref, tri_ref, mask_ref, ones_ref,
         o_ref, s_ref, st_s, q_s, k_s, kp_s, fp_s, vp_s, vb_s, b_s, gate_s,
         qc1_s, kc1_s, qc2_s, kc2_s, mix_s) = refs
    t = pl.program_id(1)
    pad = HG_SUB
    chunk = min(t_tile, HG_CHUNK)
    n1 = t_tile // chunk
    nb2 = chunk // HG_SUB
    body = slice(pad, pad + t_tile)

    @pl.when(t == 0)
    def _():
        for hd in range(HG_H):
            if has_state:
                st_s[hd] = s0_ref[hd].T
            else:
                st_s[hd] = jnp.zeros((HG_DV, HG_DK), F32)
        zpad = jnp.zeros((HG_H, pad, LANES), F32)
        kp_s[:, 0:pad, :] = zpad
        fp_s[:, 0:pad, :] = zpad
        vp_s[:, 0:pad, :] = zpad
        qc1_s[...] = jnp.zeros_like(qc1_s)
        kc1_s[...] = jnp.zeros_like(kc1_s)
        qc2_s[...] = jnp.zeros_like(qc2_s)
        kc2_s[...] = jnp.zeros_like(kc2_s)

    x = x_ref[...]
    h = _rms(x, g_ref[0:1, :]).astype(BF16)
    lb = lb_ref[...]
    q_s[...] = _silu(_dot(h, win_ref[:, 0:HG_QK]))
    f = _dot(h, win_ref[:, HG_QK:2 * HG_QK])
    fgate = lb + (1.0 - lb) * jax.nn.sigmoid(f)
    k_all = (1.0 - lb) * jax.nn.sigmoid(-f)
    k_s[...] = k_all
    v = _dot(h, win_ref[:, 2 * HG_QK:2 * HG_QK + HG_V])
    vb_s[...] = v.astype(BF16)
    for hd in range(HG_H):
        ln = slice(hd * LANES, (hd + 1) * LANES)
        fp_s[hd, body, :] = fgate[:, ln]
        kp_s[hd, body, :] = k_all[:, ln]
        vp_s[hd, body, :] = v[:, ln]
    gate_s[...] = _dot(h, win_ref[:, 2 * HG_QK + HG_V:])
    tri = tri_ref[...]
    hi, mid, low = _split3(jnp.log(fgate))
    b_s[...] = _dot(tri, hi) + _dot(tri, mid) + _dot(tri, low)
    ones2 = ones_ref[...]

    for hd in range(HG_H):
        ln = slice(hd * LANES, (hd + 1) * LANES)
        q = q_s[:, ln]
        k = k_s[:, ln]
        bc = b_s[:, ln]
        vb = vb_s[:, ln]

        for j in range(n1 - 1):
            blk = slice(j * chunk, (j + 1) * chunk)
            after = slice((j + 1) * chunk, t_tile)
            cols = slice(j * LANES, (j + 1) * LANES)
            e_j = bc[(j + 1) * chunk - 1:(j + 1) * chunk, :]
            kc1_s[blk, cols] = (k[blk] * jnp.exp(e_j - bc[blk])).astype(BF16)
            qc1_s[after, cols] = (q[after] * jnp.exp(bc[after] - e_j)).astype(BF16)
        for c in range(n1):
            for j in range(nb2 - 1):
                r0 = c * chunk + j * HG_SUB
                blk = slice(r0, r0 + HG_SUB)
                after = slice(r0 + HG_SUB, (c + 1) * chunk)
                cols = slice(j * LANES, (j + 1) * LANES)
                e_j = bc[r0 + HG_SUB - 1:r0 + HG_SUB, :]
                kc2_s[blk, cols] = k[blk] * jnp.exp(e_j - bc[blk])
                qc2_s[after, cols] = q[after] * jnp.exp(bc[after] - e_j)

        p = _dot_nt(qc2_s[...].astype(BF16), kc2_s[...].astype(BF16)) * mask_ref[1]
        if n1 > 1:
            p = p + _dot_nt(qc1_s[...], kc1_s[...]) * mask_ref[0]
        st_old = st_s[hd]
        o = _dot(p.astype(BF16), vb) + _dot_nt((q * jnp.exp(bc)).astype(BF16), st_old.astype(BF16))

        prod = None
        ws = []
        for d in range(HG_SUB):
            w = q * kp_s[hd, pad - d:pad - d + t_tile, :]
            if d >= 1:
                fsh = fp_s[hd, pad - d + 1:pad - d + 1 + t_tile, :]
                prod = fsh if d == 1 else prod * fsh
                w = w * prod
            ws.append(w.astype(BF16))
            if d % 2 == 1:
                r = _dot(jnp.concatenate(ws, axis=1), ones2)
                o = o + r[:, 0:LANES] * vp_s[hd, pad - d + 1:pad - d + 1 + t_tile, :]
                o = o + r[:, LANES:] * vp_s[hd, pad - d:pad - d + t_tile, :]
                ws = []

        b_last = bc[t_tile - 1:t_tile, :]
        k_end = (k * jnp.exp(b_last - bc)).astype(BF16)
        st_new = st_old * jnp.exp(b_last) + _dot_tn(vb, k_end)
        st_s[hd] = st_new
        s_ref[hd] = st_new.T

        o = _rms(o, gn_ref[:, ln])
        mix_s[:, ln] = (_silu(gate_s[:, ln]) * o).astype(BF16)

    y = _dot(mix_s[...], wout_ref[...])
    o_ref[...] = x + _rms(y, g_ref[1:2, :])


def _hgrn(x, g2, w_in, w_out, widx, gn, lb, s0, t_tile):
    b, l, _ = x.shape
    has_state = s0 is not None
    chunk = min(t_tile, HG_CHUNK)
    n1 = t_tile // chunk
    nb2 = chunk // HG_SUB
    tri = jnp.tril(jnp.ones((t_tile, t_tile), F32)).astype(BF16)
    n = jnp.arange(t_tile)
    far = (n[:, None] - n[None, :]) >= HG_SUB
    same = (n[:, None] // chunk) == (n[None, :] // chunk)
    masks = jnp.stack([far, far & same]).astype(F32)
    blk = jnp.arange(2 * LANES) // LANES
    ones2 = (blk[:, None] == blk[None, :]).astype(BF16)
    state_spec = pl.BlockSpec((None, HG_H, HG_DK, HG_DV), lambda i, j: (i, 0, 0, 0))
    in_specs = [
        pl.BlockSpec((None, t_tile, D_MODEL), lambda i, j: (i, j, 0)),
        _const_spec((2, D_MODEL)),
        _layer_spec(w_in, widx),
        _layer_spec(w_out, widx),
        _const_spec((1, HG_V)),
        _const_spec((1, HG_QK)),
        _const_spec(tri.shape),
        _const_spec(masks.shape),
        _const_spec(ones2.shape),
    ]
    args = [x, g2, w_in, w_out, gn, lb, tri, masks, ones2]
    if has_state:
        in_specs.append(state_spec)
        args.append(s0)
    cols1 = max(n1 - 1, 1) * LANES
    cols2 = (nb2 - 1) * LANES
    return pl.pallas_call(
        functools.partial(_hgrn_kernel, t_tile=t_tile, has_state=has_state),
        grid=(b, l // t_tile),
        in_specs=in_specs,
        out_specs=[pl.BlockSpec((None, t_tile, D_MODEL), lambda i, j: (i, j, 0)), state_spec],
        out_shape=[jax.ShapeDtypeStruct(x.shape, F32),
                   jax.ShapeDtypeStruct((b, HG_H, HG_DK, HG_DV), F32)],
        scratch_shapes=[
            pltpu.VMEM((HG_H, HG_DV, HG_DK), F32),
            pltpu.VMEM((t_tile, HG_QK), F32),
            pltpu.VMEM((t_tile, HG_QK), F32),
            pltpu.VMEM((HG_H, HG_SUB + t_tile, LANES), F32),
            pltpu.VMEM((HG_H, HG_SUB + t_tile, LANES), F32),
            pltpu.VMEM((HG_H, HG_SUB + t_tile, LANES), F32),
            pltpu.VMEM((t_tile, HG_V), BF16),
            pltpu.VMEM((t_tile, HG_QK), F32),
            pltpu.VMEM((t_tile, HG_V), F32),
            pltpu.VMEM((t_tile, cols1), BF16),
            pltpu.VMEM((t_tile, cols1), BF16),
            pltpu.VMEM((t_tile, cols2), F32),
            pltpu.VMEM((t_tile, cols2), F32),
            pltpu.VMEM((t_tile, HG_V), BF16),
        ],
        compiler_params=_params(2),
        name="hgrn2",
    )(*args)


def _rope_table(pos0, length):
    half = RET_DK // 2
    inv = ROPE_BASE ** (-jnp.arange(half, dtype=F32) / half)
    pos = pos0 + jnp.arange(length, dtype=F32)
    ang = pos[:, None] * inv[None, :]
    cos = jnp.cos(ang)
    sin = jnp.sin(ang)
    c2 = jnp.concatenate([cos, cos], axis=-1)
    s2 = jnp.concatenate([-sin, sin], axis=-1)
    ks = RET_DK ** -0.5
    return jnp.stack([c2, s2, c2 * ks, s2 * ks])


def _ret_log_decay():
    return jnp.log1p(-(2.0 ** (-5.0 - jnp.arange(RET_H, dtype=F32))))


def _ret_decay_matrix(t_tile, chunk):
    lg = _ret_log_decay()
    n = jnp.arange(t_tile)
    cn = n // chunk
    diff = (n[:, None] - n[None, :]).astype(F32)
    same = cn[:, None] == cn[None, :]
    later = cn[:, None] > cn[None, :]
    expo = jnp.where(same, jnp.abs(diff), diff)
    dec = jnp.exp(lg[:, None, None] * expo[None])
    return jnp.where((same | later)[None], dec, 0.0)


def _lower_bounds(lb_param):
    c = jnp.cumsum(jax.nn.softmax(lb_param.astype(F32), axis=0), axis=0)
    return c - c[0]


def kernel(x_prompt, x_sample, state_ret, cache_swa_k, cache_swa_v, state_hgrn, norm_g, w_ff_in, w_ff_out,
           ret_w_in, ret_w_out, ret_gn_g, swa_w_in, swa_w_out, swa_sink, hg_w_in, hg_w_out, hg_gn_g, hg_lb):
    bf = lambda w: w.astype(BF16)
    w_ff_in, w_ff_out = bf(w_ff_in), bf(w_ff_out)
    ret_w_in, ret_w_out = bf(ret_w_in), bf(ret_w_out)
    swa_w_in, swa_w_out = bf(swa_w_in), bf(swa_w_out)
    hg_w_in, hg_w_out = bf(hg_w_in), bf(hg_w_out)
    lbs = _lower_bounds(hg_lb)

    bp, lp, _ = x_prompt.shape
    bs, ls, _ = x_sample.shape
    xp = x_prompt.reshape(bp * lp, D_MODEL)
    xs = x_sample.reshape(bs * ls, D_MODEL)
    rope_p, rope_s = _rope_table(0.0, lp), _rope_table(float(PAST_LEN), ls)
    dmat_p, dmat_s = _ret_decay_matrix(MIX_TOKENS, CHUNK), _ret_decay_matrix(ls, ls)
    lg_tab = jnp.broadcast_to(_ret_log_decay()[:, None, None], (RET_H, 1, RET_DV))
    ret_p, ret_s, k_p, v_p, k_s, v_s, hg_p, hg_s = [], [], [], [], [], [], [], []

    for li in range(DEPTH):
        kind, j = li % 3, li // 3
        xp, xs = _ffn(xp, xs, norm_g[li, 0:2], w_ff_in, w_ff_out, (li, 0))
        xp3, xs3 = xp.reshape(bp, lp, D_MODEL), xs.reshape(bs, ls, D_MODEL)
        g2 = norm_g[li, 2:4]
        if kind == 0:
            gn = ret_gn_g[j][None, :]
            xp3, sp = _retention(xp3, g2, ret_w_in, ret_w_out, (j,), gn, rope_p, dmat_p, lg_tab, None,
                                 MIX_TOKENS)
            xs3, ss = _retention(xs3, g2, ret_w_in, ret_w_out, (j,), gn, rope_s, dmat_s, lg_tab,
                                 state_ret[j], ls)
            ret_p.append(sp)
            ret_s.append(ss)
        elif kind == 1:
            xp3, kn, vn = _swa_prompt(xp3, g2, swa_w_in, swa_w_out, (j,), swa_sink[j], MIX_TOKENS)
            k_p.append(kn.reshape(bp, WINDOW, SWA_HKV, SWA_DH))
            v_p.append(vn.reshape(bp, WINDOW, SWA_HKV, SWA_DH))
            kc = cache_swa_k[j].reshape(bs, WINDOW, SWA_KV)
            vc = cache_swa_v[j].reshape(bs, WINDOW, SWA_KV)
            xs3, kn, vn = _swa_decode(xs3, g2, swa_w_in, swa_w_out, (j,), swa_sink[j], kc, vc)
            k_s.append(kn.reshape(bs, ls, SWA_HKV, SWA_DH))
            v_s.append(vn.reshape(bs, ls, SWA_HKV, SWA_DH))
        else:
            gn, lb = hg_gn_g[j][None, :], lbs[li][None, :]
            xp3, sp = _hgrn(xp3, g2, hg_w_in, hg_w_out, (j,), gn, lb, None, MIX_TOKENS)
            xs3, ss = _hgrn(xs3, g2, hg_w_in, hg_w_out, (j,), gn, lb, state_hgrn[j], ls)
            hg_p.append(sp)
            hg_s.append(ss)
        xp, xs = xp3.reshape(bp * lp, D_MODEL), xs3.reshape(bs * ls, D_MODEL)
        xp, xs = _ffn(xp, xs, norm_g[li, 4:6], w_ff_in, w_ff_out, (li, 1))

    return (xp.reshape(bp, lp, D_MODEL), xs.reshape(bs, ls, D_MODEL), jnp.stack(ret_p), jnp.stack(ret_s),
            jnp.stack(k_p), jnp.stack(v_p), jnp.stack(k_s), jnp.stack(v_s), jnp.stack(hg_p), jnp.stack(hg_s))
```

```python
import functools

import jax
import jax.numpy as jnp
import numpy as np
from jax import lax
from jax.experimental import pallas as pl
from jax.experimental.pallas import tpu as pltpu

F32 = jnp.float32
BF16 = jnp.bfloat16

D_MODEL = 1024
DEPTH = 4
CHUNK = 64
PAST_LEN = 4096
D_FF = 2816
EPS = 1e-6
RET_H, RET_DK, RET_DV = 8, 128, 256
RET_QK = RET_H * RET_DK
RET_V = RET_H * RET_DV
ROPE_BASE = 10000.0
WINDOW = 128
SWA_HQ, SWA_HKV, SWA_DH = 16, 2, 64
SWA_G = SWA_HQ // SWA_HKV
SWA_Q = SWA_HQ * SWA_DH
SWA_KV = SWA_HKV * SWA_DH
HG_H, HG_DK, HG_DV = 8, 128, 128
HG_QK = HG_H * HG_DK
HG_V = HG_H * HG_DV

VMEM_LIMIT_BYTES = 56 * 1024 * 1024
LANES = 128
SUBLANES = 8
MXU_DIM = 256
FFN_ROWS = 512
MIX_TOKENS = 256
HG_CHUNK = 64
HG_SUB = SUBLANES
HG_BAND_ROWS = 32
DEC_STREAMS = 4

NT_DIMS = (((1,), (1,)), ((), ()))
TN_DIMS = (((0,), (0,)), ((), ()))


def _rms(x, g):
    return x * lax.rsqrt(jnp.mean(x * x, axis=-1, keepdims=True) + EPS) * g


def _silu(x):
    return x * jax.nn.sigmoid(x)


def _dot(a, b):
    return jnp.dot(a, b, preferred_element_type=F32)


def _dot_nt(a, b):
    return lax.dot_general(a, b, NT_DIMS, preferred_element_type=F32)


def _dot_tn(a, b):
    return lax.dot_general(a, b, TN_DIMS, preferred_element_type=F32)


def _const_spec(shape):
    n = len(shape)
    return pl.BlockSpec(shape, lambda *_: (0,) * n, pipeline_mode=pl.Buffered(1))


def _layer_spec(stacked, idx):
    tail = stacked.shape[len(idx):]
    block = (None,) * len(idx) + tail
    return pl.BlockSpec(block, lambda *_: tuple(idx) + (0,) * len(tail), pipeline_mode=pl.Buffered(1))


def _params(n_axes):
    return pltpu.CompilerParams(
        dimension_semantics=("arbitrary",) * n_axes,
        vmem_limit_bytes=VMEM_LIMIT_BYTES)


def _ffn_kernel(x_ref, g_ref, win_ref, wout_ref, o_ref, act_ref):
    x = x_ref[...]
    h = _rms(x, g_ref[0:1, :]).astype(BF16)
    for c in range(D_FF // MXU_DIM):
        cols = slice(c * MXU_DIM, (c + 1) * MXU_DIM)
        ucols = slice(D_FF + c * MXU_DIM, D_FF + (c + 1) * MXU_DIM)
        a = _dot(h, win_ref[:, cols])
        u = _dot(h, win_ref[:, ucols])
        act_ref[:, cols] = (_silu(a) * u).astype(BF16)
    y = _dot(act_ref[...], wout_ref[...])
    o_ref[...] = x + 0.5 * _rms(y, g_ref[1:2, :])


def _ffn(x, g2, w_in, w_out, widx):
    m = x.shape[0]
    tm = FFN_ROWS
    assert m % tm == 0
    row_spec = pl.BlockSpec((tm, D_MODEL), lambda i: (i, 0))
    return pl.pallas_call(
        _ffn_kernel,
        grid=(m // tm,),
        in_specs=[row_spec, _const_spec((2, D_MODEL)), _layer_spec(w_in, widx), _layer_spec(w_out, widx)],
        out_specs=row_spec,
        out_shape=jax.ShapeDtypeStruct(x.shape, F32),
        scratch_shapes=[pltpu.VMEM((tm, D_FF), BF16)],
        compiler_params=_params(1),
        name="ffn",
    )(x, g2, w_in, w_out)


def _ret_kernel(*refs, t_tile, n_streams, has_state):
    if has_state:
        (x_ref, g_ref, win_ref, wout_ref, gn_ref, rope_ref, dmat_ref, lg_ref, s0_ref,
         o_ref, s_ref, q_s, k_s, v_s, gate_s, mix_s) = refs
    else:
        (x_ref, g_ref, win_ref, wout_ref, gn_ref, rope_ref, dmat_ref, lg_ref,
         o_ref, s_ref, q_s, k_s, v_s, gate_s, mix_s) = refs
        s0_ref = None
    t = pl.program_id(1)

    @pl.when(t == 0)
    def _():
        if has_state:
            s_ref[...] = s0_ref[...]
        else:
            s_ref[...] = jnp.zeros_like(s_ref)

    x = x_ref[...]
    h = _rms(x, g_ref[0:1, :]).astype(BF16)
    q_s[...] = _dot(h, win_ref[:, 0:RET_QK])
    k_s[...] = _dot(h, win_ref[:, RET_QK:2 * RET_QK])
    v_s[...] = _dot(h, win_ref[:, 2 * RET_QK:2 * RET_QK + RET_V]).astype(BF16)
    gate_s[...] = _dot(h, win_ref[:, 2 * RET_QK + RET_V:])

    cq, sq, ck, sk = rope_ref[0], rope_ref[1], rope_ref[2], rope_ref[3]
    row = lax.broadcasted_iota(jnp.int32, (t_tile, LANES), 0).astype(F32)
    for st in range(n_streams):
        rows = slice(st * t_tile, (st + 1) * t_tile)
        for hd in range(RET_H):
            qk = slice(hd * RET_DK, (hd + 1) * RET_DK)
            vv = slice(hd * RET_DV, (hd + 1) * RET_DV)
            lg = lg_ref[hd]
            lg_k = lg[:, :RET_DK]
            qh = q_s[rows, qk]
            kh = k_s[rows, qk]
            qr = qh * cq + pltpu.roll(qh, RET_DK // 2, 1) * sq
            kr = kh * ck + pltpu.roll(kh, RET_DK // 2, 1) * sk
            vh = v_s[rows, vv]
            p = (_dot_nt(qr.astype(BF16), kr.astype(BF16)) * dmat_ref[hd]).astype(BF16)
            q_cross = (qr * jnp.exp(lg_k * (row + 1.0))).astype(BF16)
            k_dec = (kr * jnp.exp(lg_k * (float(t_tile - 1) - row))).astype(BF16)
            s_old = s_ref[st, hd]
            o = _dot(p, vh) + _dot(q_cross, s_old.astype(BF16))
            s_ref[st, hd] = jnp.exp(lg * float(t_tile)) * s_old + _dot_tn(k_dec, vh)
            o = _rms(o, gn_ref[:, vv])
            mix_s[rows, vv] = (_silu(gate_s[rows, vv]) * o).astype(BF16)
    y = _dot(mix_s[...], wout_ref[...])
    o_ref[...] = x + _rms(y, g_ref[1:2, :])


def _retention(x, g2, w_in, w_out, widx, gn, rope_tab, dmat, lg_tab, s0, t_tile, n_streams):
    b, l, _ = x.shape
    assert b % n_streams == 0 and (n_streams == 1 or l == t_tile)
    has_state = s0 is not None
    tiles = l // t_tile
    rows = n_streams * t_tile
    row_spec = pl.BlockSpec((rows, D_MODEL), lambda i, j: (i * tiles + j, 0))
    state_spec = pl.BlockSpec((n_streams, RET_H, RET_DK, RET_DV), lambda i, j: (i, 0, 0, 0))
    in_specs = [
        row_spec,
        _const_spec((2, D_MODEL)),
        _layer_spec(w_in, widx),
        _layer_spec(w_out, widx),
        _const_spec((1, RET_V)),
        pl.BlockSpec((4, t_tile, LANES), lambda i, j: (0, j, 0)),
        _const_spec(dmat.shape),
        _const_spec(lg_tab.shape),
    ]
    args = [x.reshape(b * l, D_MODEL), g2, w_in, w_out, gn, rope_tab, dmat, lg_tab]
    if has_state:
        in_specs.append(pl.BlockSpec((None, n_streams, RET_H, RET_DK, RET_DV),
                                     lambda i, j: (widx[0], i, 0, 0, 0)))
        args.append(s0)
    y, s_new = pl.pallas_call(
        functools.partial(_ret_kernel, t_tile=t_tile, n_streams=n_streams, has_state=has_state),
        grid=(b // n_streams, tiles),
        in_specs=in_specs,
        out_specs=[row_spec, state_spec],
        out_shape=[jax.ShapeDtypeStruct((b * l, D_MODEL), F32),
                   jax.ShapeDtypeStruct((b, RET_H, RET_DK, RET_DV), F32)],
        scratch_shapes=[
            pltpu.VMEM((rows, RET_QK), F32),
            pltpu.VMEM((rows, RET_QK), F32),
            pltpu.VMEM((rows, RET_V), BF16),
            pltpu.VMEM((rows, RET_V), F32),
            pltpu.VMEM((rows, RET_V), BF16),
        ],
        compiler_params=_params(2),
        name="retention",
    )(*args)
    return y.reshape(b, l, D_MODEL), s_new


def _swa_expand(a):
    lane = lax.broadcasted_iota(jnp.int32, (1, LANES), 1)
    lo = lane < SWA_DH
    ar = pltpu.roll(a, SWA_DH, 1)
    zero = jnp.zeros_like(a)
    return (jnp.where(lo, a, zero).astype(BF16), jnp.where(lo, zero, ar).astype(BF16),
            jnp.where(lo, ar, zero).astype(BF16), jnp.where(lo, zero, a).astype(BF16))


def _swa_prompt_kernel(x_ref, g_ref, win_ref, wout_ref, sink_ref, bias_ref,
                       o_ref, ko_ref, vo_ref, q_s, kx_s, vt_s, vprev_s, att_s, sc_s, ex_s, *, t_tile):
    t = pl.program_id(1)
    half = 2 * CHUNK
    n_keys = WINDOW + half
    pairs = SWA_G // 2

    @pl.when(t == 0)
    def _():
        kx_s[:, 0:WINDOW, :] = jnp.zeros((4, WINDOW, LANES), BF16)
        vprev_s[...] = jnp.zeros((4, LANES, WINDOW), BF16)

    x = x_ref[...]
    h = _rms(x, g_ref[0:1, :]).astype(BF16)
    q_s[...] = _dot(h, win_ref[:, 0:SWA_Q]) * (SWA_DH ** -0.5)
    kv = _dot(h, win_ref[:, SWA_Q:])
    k_new = kv[:, 0:SWA_KV]
    v_new = kv[:, SWA_KV:]
    ko_ref[...] = k_new[t_tile - WINDOW:, :]
    vo_ref[...] = v_new[t_tile - WINDOW:, :]
    for i, a in enumerate(_swa_expand(k_new)):
        kx_s[i, WINDOW:, :] = a
    v_t = v_new.T
    ones = jnp.ones((SWA_DH, t_tile), F32)
    for grp in range(SWA_HKV):
        vg = v_t[grp * SWA_DH:(grp + 1) * SWA_DH, :]
        vt_s[2 * grp] = jnp.concatenate([vg, ones], axis=0).astype(BF16)
        vt_s[2 * grp + 1] = jnp.concatenate([ones, vg], axis=0).astype(BF16)

    first = jnp.where(t == 0, 1, 0)
    combos = [(hc, grp, pp, par) for hc in range(t_tile // half) for grp in range(SWA_HKV)
              for pp in range(pairs // 2) for par in range(2)]
    for i, (hc, grp, pp, par) in enumerate(combos):
        keys = slice(hc * half, hc * half + n_keys)
        qrows = slice(hc * half, (hc + 1) * half)
        bias = bias_ref[first] if hc == 0 else bias_ref[0]
        pa = grp * pairs + 2 * pp
        qst = jnp.concatenate(
            [q_s[qrows, pa * LANES:(pa + 1) * LANES],
             q_s[qrows, (pa + 1) * LANES:(pa + 2) * LANES]], axis=0).astype(BF16)
        sc_s[i] = _dot_nt(kx_s[grp * 2 + par, keys, :], qst) + bias
    sink_terms = []
    for i, (hc, grp, pp, par) in enumerate(combos):
        s = sc_s[i]
        snk = sink_ref[(grp * 2 + pp) * 2 + par]
        m = jnp.maximum(jnp.max(s, axis=0, keepdims=True), snk)
        ex_s[i] = jnp.exp(s - m).astype(BF16)
        sink_terms.append(jnp.exp(snk - m))
    for i, (hc, grp, pp, par) in enumerate(combos):
        idx = grp * 2 + par
        qrows = slice(hc * half, (hc + 1) * half)
        if hc == 0:
            v_keys = jnp.concatenate([vprev_s[idx], vt_s[idx, :, 0:half]], axis=1)
        else:
            v_keys = vt_s[idx, :, hc * half - WINDOW:(hc + 1) * half]
        ov = _dot(v_keys, ex_s[i])
        o_rows = slice(par * SWA_DH, (par + 1) * SWA_DH)
        d_row = slice((1 - par) * SWA_DH, (1 - par) * SWA_DH + 1)
        out = (ov[o_rows, :] * (1.0 / (ov[d_row, :] + sink_terms[i]))).astype(BF16)
        pa = grp * pairs + 2 * pp
        for a in range(2):
            base = (pa + a) * LANES + par * SWA_DH
            att_s[base:base + SWA_DH, qrows] = out[:, a * half:(a + 1) * half]

    kx_s[:, 0:WINDOW, :] = kx_s[:, t_tile:t_tile + WINDOW, :]
    vprev_s[...] = vt_s[:, :, t_tile - WINDOW:t_tile]

    y = _dot_tn(att_s[...], wout_ref[...])
    o_ref[...] = x + _rms(y, g_ref[1:2, :])


def _swa_prompt(x, g2, w_in, w_out, widx, sink, t_tile):
    b, l, _ = x.shape
    half = 2 * CHUNK
    n_keys = WINDOW + half
    s = sink.astype(F32).reshape(SWA_HKV, SWA_G // 4, 2, 2)
    s = jnp.transpose(s, (0, 1, 3, 2)).reshape(-1, 1, 2, 1)
    sink_tab = jnp.broadcast_to(s, (s.shape[0], 1, 2, half)).reshape(-1, 1, 2 * half)
    kc = np.arange(n_keys)[:, None] // CHUNK
    qc = (np.arange(2 * half)[None, :] % half) // CHUNK
    band = (kc >= qc) & (kc <= qc + WINDOW // CHUNK)
    started = np.arange(n_keys)[:, None] >= WINDOW
    bias = np.stack([np.where(band, 0.0, -np.inf),
                     np.where(band & started, 0.0, -np.inf)]).astype(np.float32)
    kv_out = pl.BlockSpec((None, WINDOW, SWA_KV), lambda i, j: (i, 0, 0))
    return pl.pallas_call(
        functools.partial(_swa_prompt_kernel, t_tile=t_tile),
        grid=(b, l // t_tile),
        in_specs=[
            pl.BlockSpec((None, t_tile, D_MODEL), lambda i, j: (i, j, 0)),
            _const_spec((2, D_MODEL)),
            _layer_spec(w_in, widx),
            _layer_spec(w_out, widx),
            _const_spec(sink_tab.shape),
            _const_spec(bias.shape),
        ],
        out_specs=[pl.BlockSpec((None, t_tile, D_MODEL), lambda i, j: (i, j, 0)), kv_out, kv_out],
        out_shape=[jax.ShapeDtypeStruct(x.shape, F32),
                   jax.ShapeDtypeStruct((b, WINDOW, SWA_KV), F32),
                   jax.ShapeDtypeStruct((b, WINDOW, SWA_KV), F32)],
        scratch_shapes=[
            pltpu.VMEM((t_tile, SWA_Q), F32),
            pltpu.VMEM((4, WINDOW + t_tile, LANES), BF16),
            pltpu.VMEM((4, LANES, t_tile), BF16),
            pltpu.VMEM((4, LANES, WINDOW), BF16),
            pltpu.VMEM((SWA_Q, t_tile), BF16),
            pltpu.VMEM((4 * t_tile // CHUNK, n_keys, 2 * half), F32),
            pltpu.VMEM((4 * t_tile // CHUNK, n_keys, 2 * half), BF16),
        ],
        compiler_params=_params(2),
        name="swa_prompt",
    )(x, g2, w_in, w_out, sink_tab, bias)


def _swa_decode_kernel(x_ref, g_ref, win_ref, wout_ref, sink_ref, kc_ref, vc_ref,
                       o_ref, ko_ref, vo_ref, q_s, kx_s, vx_s, att_s, *, t_tile):
    pairs = SWA_G // 2
    for i, a in enumerate(_swa_expand(kc_ref[...])):
        kx_s[i, 0:WINDOW, :] = a
    for i, a in enumerate(_swa_expand(vc_ref[...])):
        vx_s[i, 0:WINDOW, :] = a

    x = x_ref[...]
    h = _rms(x, g_ref[0:1, :]).astype(BF16)
    q_s[...] = _dot(h, win_ref[:, 0:SWA_Q]) * (SWA_DH ** -0.5)
    kv = _dot(h, win_ref[:, SWA_Q:])
    k_new = kv[:, 0:SWA_KV]
    v_new = kv[:, SWA_KV:]
    ko_ref[...] = k_new
    vo_ref[...] = v_new
    for i, a in enumerate(_swa_expand(k_new)):
        kx_s[i, WINDOW:, :] = a
    for i, a in enumerate(_swa_expand(v_new)):
        vx_s[i, WINDOW:, :] = a

    for grp in range(SWA_HKV):
        qst = jnp.concatenate(
            [q_s[:, (grp * pairs + p) * LANES:(grp * pairs + p + 1) * LANES]
             for p in range(pairs)], axis=0).astype(BF16)
        out = None
        for par in range(2):
            idx = grp * 2 + par
            s = _dot_nt(qst, kx_s[idx])
            snk = sink_ref[idx][:, 0:1]
            m = jnp.maximum(jnp.max(s, axis=-1, keepdims=True), snk)
            e = jnp.exp(s - m)
            den = jnp.sum(e, axis=-1, keepdims=True) + jnp.exp(snk - m)
            pv = _dot((e / den).astype(BF16), vx_s[idx])
            out = pv if out is None else out + pv
        for p in range(pairs):
            cols = slice((grp * pairs + p) * LANES, (grp * pairs + p + 1) * LANES)
            att_s[:, cols] = out[p * t_tile:(p + 1) * t_tile, :].astype(BF16)

    y = _dot(att_s[...], wout_ref[...])
    o_ref[...] = x + _rms(y, g_ref[1:2, :])


def _swa_decode(x, g2, w_in, w_out, widx, sink, k_cache, v_cache):
    b, l, _ = x.shape
    s = sink.astype(F32).reshape(SWA_HKV, SWA_G // 2, 2)
    s = jnp.transpose(s, (0, 2, 1)).reshape(SWA_HKV * 2, SWA_G // 2)
    s = jnp.repeat(s, l, axis=1)
    sink_tab = jnp.broadcast_to(s[:, :, None], s.shape + (LANES,))
    cache_spec = pl.BlockSpec((None, WINDOW, SWA_KV), lambda i: (i, 0, 0))
    kv_out = pl.BlockSpec((None, l, SWA_KV), lambda i: (i, 0, 0))
    return pl.pallas_call(
        functools.partial(_swa_decode_kernel, t_tile=l),
        grid=(b,),
        in_specs=[
            pl.BlockSpec((None, l, D_MODEL), lambda i: (i, 0, 0)),
            _const_spec((2, D_MODEL)),
            _layer_spec(w_in, widx),
            _layer_spec(w_out, widx),
            _const_spec(sink_tab.shape),
            cache_spec, cache_spec,
        ],
        out_specs=[pl.BlockSpec((None, l, D_MODEL), lambda i: (i, 0, 0)), kv_out, kv_out],
        out_shape=[jax.ShapeDtypeStruct(x.shape, F32),
                   jax.ShapeDtypeStruct((b, l, SWA_KV), F32),
                   jax.ShapeDtypeStruct((b, l, SWA_KV), F32)],
        scratch_shapes=[
            pltpu.VMEM((l, SWA_Q), F32),
            pltpu.VMEM((4, WINDOW + l, LANES), BF16),
            pltpu.VMEM((4, WINDOW + l, LANES), BF16),
            pltpu.VMEM((l, SWA_Q), BF16),
        ],
        compiler_params=_params(1),
        name="swa_decode",
    )(x, g2, w_in, w_out, sink_tab, k_cache, v_cache)


def _split3(a):
    hi = a.astype(BF16)
    r = a - hi.astype(F32)
    mid = r.astype(BF16)
    low = (r - mid.astype(F32)).astype(BF16)
    return hi, mid, low


def _hgrn_kernel(*refs, t_tile, has_state):
    if has_state:
        (x_ref, g_ref, win_ref, wout_ref, gn_ref, lb_ref, tri_ref, mask_ref, s0_ref,
         o_ref, s_ref, st_s, q_s, kp_s, fp_s, vp_s, vb_s, b_s, gate_s,
         qc1_s, kc1_s, qc2_s, kc2_s, mix_s) = refs
    else:
        (x_ref, g_ref, win_ref, wout_ref, gn_ref, lb_ref, tri_ref, mask_ref,
         o_ref, s_ref, st_s, q_s, kp_s, fp_s, vp_s, vb_s, b_s, gate_s,
         qc1_s, kc1_s, qc2_s, kc2_s, mix_s) = refs
    t = pl.program_id(1)
    pad = HG_SUB
    chunk = min(t_tile, HG_CHUNK)
    n1 = t_tile // chunk
    nb2 = chunk // HG_SUB
    body = slice(pad, pad + t_tile)

    @pl.when(t == 0)
    def _():
        for hd in range(HG_H):
            if has_state:
                st_s[hd] = s0_ref[hd].T
            else:
                st_s[hd] = jnp.zeros((HG_DV, HG_DK), F32)
        zpad = jnp.zeros((HG_H, pad, LANES), F32)
        kp_s[:, 0:pad, :] = zpad
        fp_s[:, 0:pad, :] = zpad
        vp_s[:, 0:pad, :] = zpad
        qc1_s[...] = jnp.zeros_like(qc1_s)
        kc1_s[...] = jnp.zeros_like(kc1_s)
        qc2_s[...] = jnp.zeros_like(qc2_s)
        kc2_s[...] = jnp.zeros_like(kc2_s)

    x = x_ref[...]
    h = _rms(x, g_ref[0:1, :]).astype(BF16)
    lb = lb_ref[...]
    tri = tri_ref[...]
    hpc = MXU_DIM // LANES
    for c in range(HG_QK // MXU_DIM):
        cols = slice(c * MXU_DIM, (c + 1) * MXU_DIM)

        def proj(part, cols=cols):
            return _dot(h, win_ref[:, part * HG_QK + cols.start:part * HG_QK + cols.stop])

        lbc = lb[:, cols]
        q_s[:, cols] = _silu(proj(0))
        f = proj(1)
        fgate = lbc + (1.0 - lbc) * jax.nn.sigmoid(f)
        k_all = (1.0 - lbc) * jax.nn.sigmoid(-f)
        v = proj(2)
        vb_s[:, cols] = v.astype(BF16)
        gate_s[:, cols] = proj(3)
        for i in range(hpc):
            sub = slice(i * LANES, (i + 1) * LANES)
            fp_s[c * hpc + i, body, :] = fgate[:, sub]
            kp_s[c * hpc + i, body, :] = k_all[:, sub]
            vp_s[c * hpc + i, body, :] = v[:, sub]
        hi, mid, low = _split3(jnp.log(fgate))
        b_s[:, cols] = _dot(tri, hi) + _dot(tri, mid) + _dot(tri, low)

    for hd in range(HG_H):
        ln = slice(hd * LANES, (hd + 1) * LANES)
        q = q_s[:, ln]
        k = kp_s[hd, body, :]
        bc = b_s[:, ln]
        vb = vb_s[:, ln]

        for j in range(n1 - 1):
            blk = slice(j * chunk, (j + 1) * chunk)
            after = slice((j + 1) * chunk, t_tile)
            cols = slice(j * LANES, (j + 1) * LANES)
            e_j = bc[(j + 1) * chunk - 1:(j + 1) * chunk, :]
            kc1_s[blk, cols] = (k[blk] * jnp.exp(e_j - bc[blk])).astype(BF16)
            qc1_s[after, cols] = (q[after] * jnp.exp(bc[after] - e_j)).astype(BF16)
        for c in range(n1):
            for j in range(nb2 - 1):
                r0 = c * chunk + j * HG_SUB
                blk = slice(r0, r0 + HG_SUB)
                after = slice(r0 + HG_SUB, (c + 1) * chunk)
                cols = slice(j * LANES, (j + 1) * LANES)
                e_j = bc[r0 + HG_SUB - 1:r0 + HG_SUB, :]
                kc2_s[blk, cols] = k[blk] * jnp.exp(e_j - bc[blk])
                qc2_s[after, cols] = q[after] * jnp.exp(bc[after] - e_j)

        p = _dot_nt(qc2_s[...].astype(BF16), kc2_s[...].astype(BF16)) * mask_ref[1]
        if n1 > 1:
            p = p + _dot_nt(qc1_s[...], kc1_s[...]) * mask_ref[0]
        st_old = st_s[hd]
        o = _dot(p.astype(BF16), vb) + _dot_nt((q * jnp.exp(bc)).astype(BF16), st_old.astype(BF16))

        rows = min(t_tile, HG_BAND_ROWS)
        band = []
        for r0 in range(0, t_tile, rows):
            qb = q_s[r0:r0 + rows, ln]
            acc = None
            prod = None
            for d in range(HG_SUB):
                sh = slice(pad - d + r0, pad - d + r0 + rows)
                w = qb * kp_s[hd, sh, :]
                if d >= 1:
                    fsh = fp_s[hd, pad - d + 1 + r0:pad - d + 1 + r0 + rows, :]
                    prod = fsh if d == 1 else prod * fsh
                    w = w * prod
                term = jnp.sum(w, axis=-1, keepdims=True) * vp_s[hd, sh, :]
                acc = term if acc is None else acc + term
            band.append(acc)
        o = o + jnp.concatenate(band, axis=0)

        b_last = bc[t_tile - 1:t_tile, :]
        k_end = (k * jnp.exp(b_last - bc)).astype(BF16)
        st_new = st_old * jnp.exp(b_last) + _dot_tn(vb, k_end)
        st_s[hd] = st_new
        s_ref[hd] = st_new.T

        o = _rms(o, gn_ref[:, ln])
        mix_s[:, ln] = (_silu(gate_s[:, ln]) * o).astype(BF16)

    y = _dot(mix_s[...], wout_ref[...])
    o_ref[...] = x + _rms(y, g_ref[1:2, :])


def _hgrn(x, g2, w_in, w_out, widx, gn, lb, s0, t_tile):
    b, l, _ = x.shape
    has_state = s0 is not None
    chunk = min(t_tile, HG_CHUNK)
    n1 = t_tile // chunk
    nb2 = chunk // HG_SUB
    tri = jnp.asarray(np.tril(np.ones((t_tile, t_tile), np.float32)), dtype=BF16)
    n = np.arange(t_tile)
    far = (n[:, None] - n[None, :]) >= HG_SUB
    same = (n[:, None] // chunk) == (n[None, :] // chunk)
    masks = np.stack([far, far & same]).astype(np.float32)
    state_spec = pl.BlockSpec((None, HG_H, HG_DK, HG_DV), lambda i, j: (i, 0, 0, 0))
    in_specs = [
        pl.BlockSpec((None, t_tile, D_MODEL), lambda i, j: (i, j, 0)),
        _const_spec((2, D_MODEL)),
        _layer_spec(w_in, widx),
        _layer_spec(w_out, widx),
        _const_spec((1, HG_V)),
        _const_spec((1, HG_QK)),
        _const_spec(tri.shape),
        _const_spec(masks.shape),
    ]
    args = [x, g2, w_in, w_out, gn, lb, tri, masks]
    if has_state:
        in_specs.append(state_spec)
        args.append(s0)
    cols1 = max(n1 - 1, 1) * LANES
    cols2 = (nb2 - 1) * LANES
    return pl.pallas_call(
        functools.partial(_hgrn_kernel, t_tile=t_tile, has_state=has_state),
        grid=(b, l // t_tile),
        in_specs=in_specs,
        out_specs=[pl.BlockSpec((None, t_tile, D_MODEL), lambda i, j: (i, j, 0)), state_spec],
        out_shape=[jax.ShapeDtypeStruct(x.shape, F32),
                   jax.ShapeDtypeStruct((b, HG_H, HG_DK, HG_DV), F32)],
        scratch_shapes=[
            pltpu.VMEM((HG_H, HG_DV, HG_DK), F32),
            pltpu.VMEM((t_tile, HG_QK), F32),
            pltpu.VMEM((HG_H, HG_SUB + t_tile, LANES), F32),
            pltpu.VMEM((HG_H, HG_SUB + t_tile, LANES), F32),
            pltpu.VMEM((HG_H, HG_SUB + t_tile, LANES), F32),
            pltpu.VMEM((t_tile, HG_V), BF16),
            pltpu.VMEM((t_tile, HG_QK), F32),
            pltpu.VMEM((t_tile, HG_V), F32),
            pltpu.VMEM((t_tile, cols1), BF16),
            pltpu.VMEM((t_tile, cols1), BF16),
            pltpu.VMEM((t_tile, cols2), F32),
            pltpu.VMEM((t_tile, cols2), F32),
            pltpu.VMEM((t_tile, HG_V), BF16),
        ],
        compiler_params=_params(2),
        name="hgrn2",
    )(*args)


def _rope_table(pos0, length):
    half = RET_DK // 2
    inv = ROPE_BASE ** (-np.arange(half, dtype=np.float64) / half)
    pos = pos0 + np.arange(length, dtype=np.float64)
    ang = pos[:, None] * inv[None, :]
    cos = np.cos(ang)
    sin = np.sin(ang)
    c2 = np.concatenate([cos, cos], axis=-1)
    s2 = np.concatenate([-sin, sin], axis=-1)
    ks = RET_DK ** -0.5
    return np.stack([c2, s2, c2 * ks, s2 * ks]).astype(np.float32)


def _ret_log_decay():
    return np.log1p(-(2.0 ** (-5.0 - np.arange(RET_H, dtype=np.float64))))


def _ret_decay_matrix(t_tile, chunk):
    lg = _ret_log_decay()
    n = np.arange(t_tile)
    cn = n // chunk
    diff = (n[:, None] - n[None, :]).astype(np.float64)
    same = cn[:, None] == cn[None, :]
    later = cn[:, None] > cn[None, :]
    expo = np.where(same, np.abs(diff), diff)
    dec = np.exp(lg[:, None, None] * expo[None])
    return np.where((same | later)[None], dec, 0.0).astype(np.float32)


def _lower_bounds(lb_param):
    c = jnp.cumsum(jax.nn.softmax(lb_param.astype(F32), axis=0), axis=0)
    return c - c[0]


def kernel(x_prompt, x_sample, state_ret, cache_swa_k, cache_swa_v, state_hgrn, norm_g, w_ff_in, w_ff_out,
           ret_w_in, ret_w_out, ret_gn_g, swa_w_in, swa_w_out, swa_sink, hg_w_in, hg_w_out, hg_gn_g, hg_lb):
    bf = lambda w: w.astype(BF16)
    w_ff_in, w_ff_out = bf(w_ff_in), bf(w_ff_out)
    ret_w_in, ret_w_out = bf(ret_w_in), bf(ret_w_out)
    swa_w_in, swa_w_out = bf(swa_w_in), bf(swa_w_out)
    hg_w_in, hg_w_out = bf(hg_w_in), bf(hg_w_out)
    lbs = _lower_bounds(hg_lb)

    bp, lp, _ = x_prompt.shape
    bs, ls, _ = x_sample.shape
    xp = x_prompt.reshape(bp * lp, D_MODEL)
    xs = x_sample.reshape(bs * ls, D_MODEL)
    rope_p, rope_s = _rope_table(0.0, lp), _rope_table(float(PAST_LEN), ls)
    dmat_p, dmat_s = _ret_decay_matrix(MIX_TOKENS, CHUNK), _ret_decay_matrix(ls, ls)
    lg_tab = np.broadcast_to(_ret_log_decay().astype(np.float32)[:, None, None], (RET_H, 1, RET_DV))
    ret_p, ret_s, k_p, v_p, k_s, v_s, hg_p, hg_s = [], [], [], [], [], [], [], []

    for li in range(DEPTH):
        kind, j = li % 3, li // 3
        xp = _ffn(xp, norm_g[li, 0:2], w_ff_in, w_ff_out, (li, 0))
        xs = _ffn(xs, norm_g[li, 0:2], w_ff_in, w_ff_out, (li, 0))
        xp3, xs3 = xp.reshape(bp, lp, D_MODEL), xs.reshape(bs, ls, D_MODEL)
        g2 = norm_g[li, 2:4]
        if kind == 0:
            gn = ret_gn_g[j][None, :]
            xp3, sp = _retention(xp3, g2, ret_w_in, ret_w_out, (j,), gn, rope_p, dmat_p, lg_tab, None,
                                 MIX_TOKENS, 1)
            xs3, ss = _retention(xs3, g2, ret_w_in, ret_w_out, (j,), gn, rope_s, dmat_s, lg_tab,
                                 state_ret, ls, DEC_STREAMS)
            ret_p.append(sp)
            ret_s.append(ss)
        elif kind == 1:
            xp3, kn, vn = _swa_prompt(xp3, g2, swa_w_in, swa_w_out, (j,), swa_sink[j], MIX_TOKENS)
            k_p.append(kn.reshape(bp, WINDOW, SWA_HKV, SWA_DH))
            v_p.append(vn.reshape(bp, WINDOW, SWA_HKV, SWA_DH))
            kc = cache_swa_k[j].reshape(bs, WINDOW, SWA_KV)
            vc = cache_swa_v[j].reshape(bs, WINDOW, SWA_KV)
            xs3, kn, vn = _swa_decode(xs3, g2, swa_w_in, swa_w_out, (j,), swa_sink[j], kc, vc)
            k_s.append(kn.reshape(bs, ls, SWA_HKV, SWA_DH))
            v_s.append(vn.reshape(bs, ls, SWA_HKV, SWA_DH))
        else:
            gn, lb = hg_gn_g[j][None, :], lbs[li][None, :]
            xp3, sp = _hgrn(xp3, g2, hg_w_in, hg_w_out, (j,), gn, lb, None, MIX_TOKENS)
            xs3, ss = _hgrn(xs3, g2, hg_w_in, hg_w_out, (j,), gn, lb, state_hgrn[j], ls)
            hg_p.append(sp)
            hg_s.append(ss)
        xp, xs = xp3.reshape(bp * lp, D_MODEL), xs3.reshape(bs * ls, D_MODEL)
        xp = _ffn(xp, norm_g[li, 4:6], w_ff_in, w_ff_out, (li, 1))
        xs = _ffn(xs, norm_g[li, 4:6], w_ff_in, w_ff_out, (li, 1))

    return (xp.reshape(bp, lp, D_MODEL), xs.reshape(bs, ls, D_MODEL), jnp.stack(ret_p), jnp.stack(ret_s),
            jnp.stack(k_p), jnp.stack(v_p), jnp.stack(k_s), jnp.stack(v_s), jnp.stack(hg_p), jnp.stack(hg_s))
```

```python
import functools

import jax
import jax.numpy as jnp
import numpy as np
from jax import lax
from jax.experimental import pallas as pl
from jax.experimental.pallas import tpu as pltpu

F32 = jnp.float32
BF16 = jnp.bfloat16

D_MODEL = 1024
DEPTH = 4
CHUNK = 64
PAST_LEN = 4096
D_FF = 2816
EPS = 1e-6
RET_H, RET_DK, RET_DV = 8, 128, 256
RET_QK = RET_H * RET_DK
RET_V = RET_H * RET_DV
ROPE_BASE = 10000.0
WINDOW = 128
SWA_HQ, SWA_HKV, SWA_DH = 16, 2, 64
SWA_G = SWA_HQ // SWA_HKV
SWA_Q = SWA_HQ * SWA_DH
SWA_KV = SWA_HKV * SWA_DH
HG_H, HG_DK, HG_DV = 8, 128, 128
HG_QK = HG_H * HG_DK
HG_V = HG_H * HG_DV

VMEM_LIMIT_BYTES = 56 * 1024 * 1024
LANES = 128
SUBLANES = 8
MXU_DIM = 256
FFN_ROWS = 1024
MIX_TOKENS = 256
RET_SUBTILES = 2
SWA_TOKENS = 512
HG_CHUNK = 64
HG_SUB = SUBLANES
HG_BAND_ROWS = 32
DEC_STREAMS = 4

NT_DIMS = (((1,), (1,)), ((), ()))
TN_DIMS = (((0,), (0,)), ((), ()))


def _rms(x, g):
    return x * lax.rsqrt(jnp.mean(x * x, axis=-1, keepdims=True) + EPS) * g


def _silu(x):
    return x * jax.nn.sigmoid(x)


def _dot(a, b):
    return jnp.dot(a, b, preferred_element_type=F32)


def _dot_nt(a, b):
    return lax.dot_general(a, b, NT_DIMS, preferred_element_type=F32)


def _dot_tn(a, b):
    return lax.dot_general(a, b, TN_DIMS, preferred_element_type=F32)


def _const_spec(shape):
    n = len(shape)
    return pl.BlockSpec(shape, lambda *_: (0,) * n, pipeline_mode=pl.Buffered(1))


def _layer_spec(stacked, idx):
    tail = stacked.shape[len(idx):]
    block = (None,) * len(idx) + tail
    return pl.BlockSpec(block, lambda *_: tuple(idx) + (0,) * len(tail), pipeline_mode=pl.Buffered(1))


def _params(n_axes):
    return pltpu.CompilerParams(
        dimension_semantics=("arbitrary",) * n_axes,
        vmem_limit_bytes=VMEM_LIMIT_BYTES)


def _ffn_kernel(x_ref, g_ref, win_ref, wout_ref, o_ref, act_ref):
    x = x_ref[...]
    h = _rms(x, g_ref[0:1, :]).astype(BF16)
    for c in range(D_FF // MXU_DIM):
        cols = slice(c * MXU_DIM, (c + 1) * MXU_DIM)
        ucols = slice(D_FF + c * MXU_DIM, D_FF + (c + 1) * MXU_DIM)
        a = _dot(h, win_ref[:, cols])
        u = _dot(h, win_ref[:, ucols])
        act_ref[:, cols] = (_silu(a) * u).astype(BF16)
    y = _dot(act_ref[...], wout_ref[...])
    o_ref[...] = x + 0.5 * _rms(y, g_ref[1:2, :])


def _ffn(x, g2, w_in, w_out, widx):
    m = x.shape[0]
    tm = min(FFN_ROWS, m)
    assert m % tm == 0
    row_spec = pl.BlockSpec((tm, D_MODEL), lambda i: (i, 0))
    return pl.pallas_call(
        _ffn_kernel,
        grid=(m // tm,),
        in_specs=[row_spec, _const_spec((2, D_MODEL)), _layer_spec(w_in, widx), _layer_spec(w_out, widx)],
        out_specs=row_spec,
        out_shape=jax.ShapeDtypeStruct(x.shape, F32),
        scratch_shapes=[pltpu.VMEM((tm, D_FF), BF16)],
        compiler_params=_params(1),
        name="ffn",
    )(x, g2, w_in, w_out)


def _ret_kernel(*refs, t_tile, n_streams, n_sub, has_state):
    if has_state:
        (x_ref, g_ref, win_ref, wout_ref, gn_ref, rope_ref, dmat_ref, lg_ref, s0_ref,
         o_ref, s_ref, q_s, k_s, v_s, gate_s, mix_s) = refs
    else:
        (x_ref, g_ref, win_ref, wout_ref, gn_ref, rope_ref, dmat_ref, lg_ref,
         o_ref, s_ref, q_s, k_s, v_s, gate_s, mix_s) = refs
        s0_ref = None
    t = pl.program_id(1)

    @pl.when(t == 0)
    def _():
        if has_state:
            s_ref[...] = s0_ref[...]
        else:
            s_ref[...] = jnp.zeros_like(s_ref)

    x = x_ref[...]
    h = _rms(x, g_ref[0:1, :]).astype(BF16)
    q_s[...] = _dot(h, win_ref[:, 0:RET_QK])
    k_s[...] = _dot(h, win_ref[:, RET_QK:2 * RET_QK])
    v_s[...] = _dot(h, win_ref[:, 2 * RET_QK:2 * RET_QK + RET_V]).astype(BF16)
    gate_s[...] = _dot(h, win_ref[:, 2 * RET_QK + RET_V:])

    row = lax.broadcasted_iota(jnp.int32, (t_tile, LANES), 0).astype(F32)
    for st, sub in [(a, b) for a in range(n_streams) for b in range(n_sub)]:
        rows = slice((st * n_sub + sub) * t_tile, (st * n_sub + sub + 1) * t_tile)
        pos = slice(sub * t_tile, (sub + 1) * t_tile)
        cq, sq, ck, sk = rope_ref[0, pos, :], rope_ref[1, pos, :], rope_ref[2, pos, :], rope_ref[3, pos, :]
        for hd in range(RET_H):
            qk = slice(hd * RET_DK, (hd + 1) * RET_DK)
            vv = slice(hd * RET_DV, (hd + 1) * RET_DV)
            lg = lg_ref[hd]
            lg_k = lg[:, :RET_DK]
            qh = q_s[rows, qk]
            kh = k_s[rows, qk]
            qr = qh * cq + pltpu.roll(qh, RET_DK // 2, 1) * sq
            kr = kh * ck + pltpu.roll(kh, RET_DK // 2, 1) * sk
            vh = v_s[rows, vv]
            p = (_dot_nt(qr.astype(BF16), kr.astype(BF16)) * dmat_ref[hd]).astype(BF16)
            q_cross = (qr * jnp.exp(lg_k * (row + 1.0))).astype(BF16)
            k_dec = (kr * jnp.exp(lg_k * (float(t_tile - 1) - row))).astype(BF16)
            s_old = s_ref[st, hd]
            o = _dot(p, vh) + _dot(q_cross, s_old.astype(BF16))
            s_ref[st, hd] = jnp.exp(lg * float(t_tile)) * s_old + _dot_tn(k_dec, vh)
            o = _rms(o, gn_ref[:, vv])
            mix_s[rows, vv] = (_silu(gate_s[rows, vv]) * o).astype(BF16)
    y = _dot(mix_s[...], wout_ref[...])
    o_ref[...] = x + _rms(y, g_ref[1:2, :])


def _retention(x, g2, w_in, w_out, widx, gn, rope_tab, dmat, lg_tab, s0, t_tile, n_streams, n_sub):
    b, l, _ = x.shape
    step_tokens = t_tile * n_sub
    assert b % n_streams == 0 and l % step_tokens == 0 and (n_streams == 1 or l == step_tokens)
    has_state = s0 is not None
    tiles = l // step_tokens
    rows = n_streams * step_tokens
    row_spec = pl.BlockSpec((rows, D_MODEL), lambda i, j: (i * tiles + j, 0))
    state_spec = pl.BlockSpec((n_streams, RET_H, RET_DK, RET_DV), lambda i, j: (i, 0, 0, 0))
    in_specs = [
        row_spec,
        _const_spec((2, D_MODEL)),
        _layer_spec(w_in, widx),
        _layer_spec(w_out, widx),
        _const_spec((1, RET_V)),
        pl.BlockSpec((4, step_tokens, LANES), lambda i, j: (0, j, 0)),
        _const_spec(dmat.shape),
        _const_spec(lg_tab.shape),
    ]
    args = [x.reshape(b * l, D_MODEL), g2, w_in, w_out, gn, rope_tab, dmat, lg_tab]
    if has_state:
        in_specs.append(pl.BlockSpec((None, n_streams, RET_H, RET_DK, RET_DV),
                                     lambda i, j: (widx[0], i, 0, 0, 0)))
        args.append(s0)
    y, s_new = pl.pallas_call(
        functools.partial(_ret_kernel, t_tile=t_tile, n_streams=n_streams, n_sub=n_sub, has_state=has_state),
        grid=(b // n_streams, tiles),
        in_specs=in_specs,
        out_specs=[row_spec, state_spec],
        out_shape=[jax.ShapeDtypeStruct((b * l, D_MODEL), F32),
                   jax.ShapeDtypeStruct((b, RET_H, RET_DK, RET_DV), F32)],
        scratch_shapes=[
            pltpu.VMEM((rows, RET_QK), F32),
            pltpu.VMEM((rows, RET_QK), F32),
            pltpu.VMEM((rows, RET_V), BF16),
            pltpu.VMEM((rows, RET_V), F32),
            pltpu.VMEM((rows, RET_V), BF16),
        ],
        compiler_params=_params(2),
        name="retention",
    )(*args)
    return y.reshape(b, l, D_MODEL), s_new


def _swa_expand(a):
    lane = lax.broadcasted_iota(jnp.int32, (1, LANES), 1)
    lo = lane < SWA_DH
    ar = pltpu.roll(a, SWA_DH, 1)
    zero = jnp.zeros_like(a)
    return (jnp.where(lo, a, zero).astype(BF16), jnp.where(lo, zero, ar).astype(BF16),
            jnp.where(lo, ar, zero).astype(BF16), jnp.where(lo, zero, a).astype(BF16))


def _swa_prompt_kernel(x_ref, g_ref, win_ref, wout_ref, sink_ref, bias_ref,
                       o_ref, ko_ref, vo_ref, q_s, kx_s, vt_s, vprev_s, att_s, sc_s, ex_s, *, t_tile):
    t = pl.program_id(1)
    half = 2 * CHUNK
    n_keys = WINDOW + half
    pairs = SWA_G // 2

    @pl.when(t == 0)
    def _():
        kx_s[:, 0:WINDOW, :] = jnp.zeros((4, WINDOW, LANES), BF16)
        vprev_s[...] = jnp.zeros((4, LANES, WINDOW), BF16)

    x = x_ref[...]
    h = _rms(x, g_ref[0:1, :]).astype(BF16)
    q_s[...] = _dot(h, win_ref[:, 0:SWA_Q]) * (SWA_DH ** -0.5)
    kv = _dot(h, win_ref[:, SWA_Q:])
    k_new = kv[:, 0:SWA_KV]
    v_new = kv[:, SWA_KV:]
    ko_ref[...] = k_new[t_tile - WINDOW:, :]
    vo_ref[...] = v_new[t_tile - WINDOW:, :]
    for i, a in enumerate(_swa_expand(k_new)):
        kx_s[i, WINDOW:, :] = a
    v_t = v_new.T
    ones = jnp.ones((SWA_DH, t_tile), F32)
    for grp in range(SWA_HKV):
        vg = v_t[grp * SWA_DH:(grp + 1) * SWA_DH, :]
        vt_s[2 * grp] = jnp.concatenate([vg, ones], axis=0).astype(BF16)
        vt_s[2 * grp + 1] = jnp.concatenate([ones, vg], axis=0).astype(BF16)

    first = jnp.where(t == 0, 1, 0)
    combos = [(hc, grp, pp, par) for hc in range(t_tile // half) for grp in range(SWA_HKV)
              for pp in range(pairs // 2) for par in range(2)]
    for i, (hc, grp, pp, par) in enumerate(combos):
        keys = slice(hc * half, hc * half + n_keys)
        qrows = slice(hc * half, (hc + 1) * half)
        bias = bias_ref[first] if hc == 0 else bias_ref[0]
        pa = grp * pairs + 2 * pp
        qst = jnp.concatenate(
            [q_s[qrows, pa * LANES:(pa + 1) * LANES],
             q_s[qrows, (pa + 1) * LANES:(pa + 2) * LANES]], axis=0).astype(BF16)
        sc_s[i] = _dot_nt(kx_s[grp * 2 + par, keys, :], qst) + bias
    sink_terms = []
    for i, (hc, grp, pp, par) in enumerate(combos):
        s = sc_s[i]
        snk = sink_ref[(grp * 2 + pp) * 2 + par]
        m = jnp.maximum(jnp.max(s, axis=0, keepdims=True), snk)
        ex_s[i] = jnp.exp(s - m).astype(BF16)
        sink_terms.append(jnp.exp(snk - m))
    for i, (hc, grp, pp, par) in enumerate(combos):
        idx = grp * 2 + par
        qrows = slice(hc * half, (hc + 1) * half)
        if hc == 0:
            v_keys = jnp.concatenate([vprev_s[idx], vt_s[idx, :, 0:half]], axis=1)
        else:
            v_keys = vt_s[idx, :, hc * half - WINDOW:(hc + 1) * half]
        ov = _dot(v_keys, ex_s[i])
        o_rows = slice(par * SWA_DH, (par + 1) * SWA_DH)
        d_row = slice((1 - par) * SWA_DH, (1 - par) * SWA_DH + 1)
        out = (ov[o_rows, :] * (1.0 / (ov[d_row, :] + sink_terms[i]))).astype(BF16)
        pa = grp * pairs + 2 * pp
        for a in range(2):
            base = (pa + a) * LANES + par * SWA_DH
            att_s[base:base + SWA_DH, qrows] = out[:, a * half:(a + 1) * half]

    kx_s[:, 0:WINDOW, :] = kx_s[:, t_tile:t_tile + WINDOW, :]
    vprev_s[...] = vt_s[:, :, t_tile - WINDOW:t_tile]

    y = _dot_tn(att_s[...], wout_ref[...])
    o_ref[...] = x + _rms(y, g_ref[1:2, :])


def _swa_prompt(x, g2, w_in, w_out, widx, sink, t_tile):
    b, l, _ = x.shape
    half = 2 * CHUNK
    n_keys = WINDOW + half
    s = sink.astype(F32).reshape(SWA_HKV, SWA_G // 4, 2, 2)
    s = jnp.transpose(s, (0, 1, 3, 2)).reshape(-1, 1, 2, 1)
    sink_tab = jnp.broadcast_to(s, (s.shape[0], 1, 2, half)).reshape(-1, 1, 2 * half)
    kc = np.arange(n_keys)[:, None] // CHUNK
    qc = (np.arange(2 * half)[None, :] % half) // CHUNK
    band = (kc >= qc) & (kc <= qc + WINDOW // CHUNK)
    started = np.arange(n_keys)[:, None] >= WINDOW
    bias = np.stack([np.where(band, 0.0, -np.inf),
                     np.where(band & started, 0.0, -np.inf)]).astype(np.float32)
    kv_out = pl.BlockSpec((None, WINDOW, SWA_KV), lambda i, j: (i, 0, 0))
    return pl.pallas_call(
        functools.partial(_swa_prompt_kernel, t_tile=t_tile),
        grid=(b, l // t_tile),
        in_specs=[
            pl.BlockSpec((None, t_tile, D_MODEL), lambda i, j: (i, j, 0)),
            _const_spec((2, D_MODEL)),
            _layer_spec(w_in, widx),
            _layer_spec(w_out, widx),
            _const_spec(sink_tab.shape),
            _const_spec(bias.shape),
        ],
        out_specs=[pl.BlockSpec((None, t_tile, D_MODEL), lambda i, j: (i, j, 0)), kv_out, kv_out],
        out_shape=[jax.ShapeDtypeStruct(x.shape, F32),
                   jax.ShapeDtypeStruct((b, WINDOW, SWA_KV), F32),
                   jax.ShapeDtypeStruct((b, WINDOW, SWA_KV), F32)],
        scratch_shapes=[
            pltpu.VMEM((t_tile, SWA_Q), F32),
            pltpu.VMEM((4, WINDOW + t_tile, LANES), BF16),
            pltpu.VMEM((4, LANES, t_tile), BF16),
            pltpu.VMEM((4, LANES, WINDOW), BF16),
            pltpu.VMEM((SWA_Q, t_tile), BF16),
            pltpu.VMEM((4 * t_tile // CHUNK, n_keys, 2 * half), F32),
            pltpu.VMEM((4 * t_tile // CHUNK, n_keys, 2 * half), BF16),
        ],
        compiler_params=_params(2),
        name="swa_prompt",
    )(x, g2, w_in, w_out, sink_tab, bias)


def _swa_decode_kernel(x_ref, g_ref, win_ref, wout_ref, sink_ref, kc_ref, vc_ref,
                       o_ref, ko_ref, vo_ref, q_s, kx_s, vx_s, att_s, *, t_tile):
    pairs = SWA_G // 2
    for i, a in enumerate(_swa_expand(kc_ref[...])):
        kx_s[i, 0:WINDOW, :] = a
    for i, a in enumerate(_swa_expand(vc_ref[...])):
        vx_s[i, 0:WINDOW, :] = a

    x = x_ref[...]
    h = _rms(x, g_ref[0:1, :]).astype(BF16)
    q_s[...] = _dot(h, win_ref[:, 0:SWA_Q]) * (SWA_DH ** -0.5)
    kv = _dot(h, win_ref[:, SWA_Q:])
    k_new = kv[:, 0:SWA_KV]
    v_new = kv[:, SWA_KV:]
    ko_ref[...] = k_new
    vo_ref[...] = v_new
    for i, a in enumerate(_swa_expand(k_new)):
        kx_s[i, WINDOW:, :] = a
    for i, a in enumerate(_swa_expand(v_new)):
        vx_s[i, WINDOW:, :] = a

    for grp in range(SWA_HKV):
        qst = jnp.concatenate(
            [q_s[:, (grp * pairs + p) * LANES:(grp * pairs + p + 1) * LANES]
             for p in range(pairs)], axis=0).astype(BF16)
        out = None
        for par in range(2):
            idx = grp * 2 + par
            s = _dot_nt(qst, kx_s[idx])
            snk = sink_ref[idx][:, 0:1]
            m = jnp.maximum(jnp.max(s, axis=-1, keepdims=True), snk)
            e = jnp.exp(s - m)
            den = jnp.sum(e, axis=-1, keepdims=True) + jnp.exp(snk - m)
            pv = _dot((e / den).astype(BF16), vx_s[idx])
            out = pv if out is None else out + pv
        for p in range(pairs):
            cols = slice((grp * pairs + p) * LANES, (grp * pairs + p + 1) * LANES)
            att_s[:, cols] = out[p * t_tile:(p + 1) * t_tile, :].astype(BF16)

    y = _dot(att_s[...], wout_ref[...])
    o_ref[...] = x + _rms(y, g_ref[1:2, :])


def _swa_decode(x, g2, w_in, w_out, widx, sink, k_cache, v_cache):
    b, l, _ = x.shape
    s = sink.astype(F32).reshape(SWA_HKV, SWA_G // 2, 2)
    s = jnp.transpose(s, (0, 2, 1)).reshape(SWA_HKV * 2, SWA_G // 2)
    s = jnp.repeat(s, l, axis=1)
    sink_tab = jnp.broadcast_to(s[:, :, None], s.shape + (LANES,))
    cache_spec = pl.BlockSpec((None, WINDOW, SWA_KV), lambda i: (i, 0, 0))
    kv_out = pl.BlockSpec((None, l, SWA_KV), lambda i: (i, 0, 0))
    return pl.pallas_call(
        functools.partial(_swa_decode_kernel, t_tile=l),
        grid=(b,),
        in_specs=[
            pl.BlockSpec((None, l, D_MODEL), lambda i: (i, 0, 0)),
            _const_spec((2, D_MODEL)),
            _layer_spec(w_in, widx),
            _layer_spec(w_out, widx),
            _const_spec(sink_tab.shape),
            cache_spec, cache_spec,
        ],
        out_specs=[pl.BlockSpec((None, l, D_MODEL), lambda i: (i, 0, 0)), kv_out, kv_out],
        out_shape=[jax.ShapeDtypeStruct(x.shape, F32),
                   jax.ShapeDtypeStruct((b, l, SWA_KV), F32),
                   jax.ShapeDtypeStruct((b, l, SWA_KV), F32)],
        scratch_shapes=[
            pltpu.VMEM((l, SWA_Q), F32),
            pltpu.VMEM((4, WINDOW + l, LANES), BF16),
            pltpu.VMEM((4, WINDOW + l, LANES), BF16),
            pltpu.VMEM((l, SWA_Q), BF16),
        ],
        compiler_params=_params(1),
        name="swa_decode",
    )(x, g2, w_in, w_out, sink_tab, k_cache, v_cache)


def _split3(a):
    hi = a.astype(BF16)
    r = a - hi.astype(F32)
    mid = r.astype(BF16)
    low = (r - mid.astype(F32)).astype(BF16)
    return hi, mid, low


def _hgrn_kernel(*refs, t_tile, has_state):
    if has_state:
        (x_ref, g_ref, win_ref, wout_ref, gn_ref, lb_ref, tri_ref, mask_ref, s0_ref,
         o_ref, s_ref, st_s, q_s, kp_s, fp_s, vp_s, vb_s, b_s, gate_s,
         qc1_s, kc1_s, qc2_s, kc2_s, mix_s) = refs
    else:
        (x_ref, g_ref, win_ref, wout_ref, gn_ref, lb_ref, tri_ref, mask_ref,
         o_ref, s_ref, st_s, q_s, kp_s, fp_s, vp_s, vb_s, b_s, gate_s,
         qc1_s, kc1_s, qc2_s, kc2_s, mix_s) = refs
    t = pl.program_id(1)
    pad = HG_SUB
    chunk = min(t_tile, HG_CHUNK)
    n1 = t_tile // chunk
    nb2 = chunk // HG_SUB
    body = slice(pad, pad + t_tile)

    @pl.when(t == 0)
    def _():
        for hd in range(HG_H):
            if has_state:
                st_s[hd] = s0_ref[hd].T
            else:
                st_s[hd] = jnp.zeros((HG_DV, HG_DK), F32)
        zpad = jnp.zeros((HG_H, pad, LANES), F32)
        kp_s[:, 0:pad, :] = zpad
        fp_s[:, 0:pad, :] = zpad
        vp_s[:, 0:pad, :] = zpad
        qc1_s[...] = jnp.zeros_like(qc1_s)
        kc1_s[...] = jnp.zeros_like(kc1_s)
        qc2_s[...] = jnp.zeros_like(qc2_s)
        kc2_s[...] = jnp.zeros_like(kc2_s)

    x = x_ref[...]
    h = _rms(x, g_ref[0:1, :]).astype(BF16)
    lb = lb_ref[...]
    tri = tri_ref[...]
    hpc = MXU_DIM // LANES
    for c in range(HG_QK // MXU_DIM):
        cols = slice(c * MXU_DIM, (c + 1) * MXU_DIM)

        def proj(part, cols=cols):
            return _dot(h, win_ref[:, part * HG_QK + cols.start:part * HG_QK + cols.stop])

        lbc = lb[:, cols]
        q_s[:, cols] = _silu(proj(0))
        f = proj(1)
        fgate = lbc + (1.0 - lbc) * jax.nn.sigmoid(f)
        k_all = (1.0 - lbc) * jax.nn.sigmoid(-f)
        v = proj(2)
        vb_s[:, cols] = v.astype(BF16)
        gate_s[:, cols] = proj(3)
        for i in range(hpc):
            sub = slice(i * LANES, (i + 1) * LANES)
            fp_s[c * hpc + i, body, :] = fgate[:, sub]
            kp_s[c * hpc + i, body, :] = k_all[:, sub]
            vp_s[c * hpc + i, body, :] = v[:, sub]
        hi, mid, low = _split3(jnp.log(fgate))
        b_s[:, cols] = _dot(tri, hi) + _dot(tri, mid) + _dot(tri, low)

    for hd in range(HG_H):
        ln = slice(hd * LANES, (hd + 1) * LANES)
        q = q_s[:, ln]
        k = kp_s[hd, body, :]
        bc = b_s[:, ln]
        vb = vb_s[:, ln]

        for j in range(n1 - 1):
            blk = slice(j * chunk, (j + 1) * chunk)
            after = slice((j + 1) * chunk, t_tile)
            cols = slice(j * LANES, (j + 1) * LANES)
            e_j = bc[(j + 1) * chunk - 1:(j + 1) * chunk, :]
            kc1_s[blk, cols] = (k[blk] * jnp.exp(e_j - bc[blk])).astype(BF16)
            qc1_s[after, cols] = (q[after] * jnp.exp(bc[after] - e_j)).astype(BF16)
        for c in range(n1):
            for j in range(nb2 - 1):
                r0 = c * chunk + j * HG_SUB
                blk = slice(r0, r0 + HG_SUB)
                after = slice(r0 + HG_SUB, (c + 1) * chunk)
                cols = slice(j * LANES, (j + 1) * LANES)
                e_j = bc[r0 + HG_SUB - 1:r0 + HG_SUB, :]
                kc2_s[blk, cols] = k[blk] * jnp.exp(e_j - bc[blk])
                qc2_s[after, cols] = q[after] * jnp.exp(bc[after] - e_j)

        p = _dot_nt(qc2_s[...].astype(BF16), kc2_s[...].astype(BF16)) * mask_ref[1]
        if n1 > 1:
            p = p + _dot_nt(qc1_s[...], kc1_s[...]) * mask_ref[0]
        st_old = st_s[hd]
        o = _dot(p.astype(BF16), vb) + _dot_nt((q * jnp.exp(bc)).astype(BF16), st_old.astype(BF16))

        rows = min(t_tile, HG_BAND_ROWS)
        band = []
        for r0 in range(0, t_tile, rows):
            qb = q_s[r0:r0 + rows, ln]
            acc = None
            prod = None
            for d in range(HG_SUB):
                sh = slice(pad - d + r0, pad - d + r0 + rows)
                w = qb * kp_s[hd, sh, :]
                if d >= 1:
                    fsh = fp_s[hd, pad - d + 1 + r0:pad - d + 1 + r0 + rows, :]
                    prod = fsh if d == 1 else prod * fsh
                    w = w * prod
                term = jnp.sum(w, axis=-1, keepdims=True) * vp_s[hd, sh, :]
                acc = term if acc is None else acc + term
            band.append(acc)
        o = o + jnp.concatenate(band, axis=0)

        b_last = bc[t_tile - 1:t_tile, :]
        k_end = (k * jnp.exp(b_last - bc)).astype(BF16)
        st_new = st_old * jnp.exp(b_last) + _dot_tn(vb, k_end)
        st_s[hd] = st_new
        s_ref[hd] = st_new.T

        o = _rms(o, gn_ref[:, ln])
        mix_s[:, ln] = (_silu(gate_s[:, ln]) * o).astype(BF16)

    y = _dot(mix_s[...], wout_ref[...])
    o_ref[...] = x + _rms(y, g_ref[1:2, :])


def _hgrn(x, g2, w_in, w_out, widx, gn, lb, s0, t_tile):
    b, l, _ = x.shape
    has_state = s0 is not None
    chunk = min(t_tile, HG_CHUNK)
    n1 = t_tile // chunk
    nb2 = chunk // HG_SUB
    tri = jnp.asarray(np.tril(np.ones((t_tile, t_tile), np.float32)), dtype=BF16)
    n = np.arange(t_tile)
    far = (n[:, None] - n[None, :]) >= HG_SUB
    same = (n[:, None] // chunk) == (n[None, :] // chunk)
    masks = np.stack([far, far & same]).astype(np.float32)
    state_spec = pl.BlockSpec((None, HG_H, HG_DK, HG_DV), lambda i, j: (i, 0, 0, 0))
    in_specs = [
        pl.BlockSpec((None, t_tile, D_MODEL), lambda i, j: (i, j, 0)),
        _const_spec((2, D_MODEL)),
        _layer_spec(w_in, widx),
        _layer_spec(w_out, widx),
        _const_spec((1, HG_V)),
        _const_spec((1, HG_QK)),
        _const_spec(tri.shape),
        _const_spec(masks.shape),
    ]
    args = [x, g2, w_in, w_out, gn, lb, tri, masks]
    if has_state:
        in_specs.append(state_spec)
        args.append(s0)
    cols1 = max(n1 - 1, 1) * LANES
    cols2 = (nb2 - 1) * LANES
    return pl.pallas_call(
        functools.partial(_hgrn_kernel, t_tile=t_tile, has_state=has_state),
        grid=(b, l // t_tile),
        in_specs=in_specs,
        out_specs=[pl.BlockSpec((None, t_tile, D_MODEL), lambda i, j: (i, j, 0)), state_spec],
        out_shape=[jax.ShapeDtypeStruct(x.shape, F32),
                   jax.ShapeDtypeStruct((b, HG_H, HG_DK, HG_DV), F32)],
        scratch_shapes=[
            pltpu.VMEM((HG_H, HG_DV, HG_DK), F32),
            pltpu.VMEM((t_tile, HG_QK), F32),
            pltpu.VMEM((HG_H, HG_SUB + t_tile, LANES), F32),
            pltpu.VMEM((HG_H, HG_SUB + t_tile, LANES), F32),
            pltpu.VMEM((HG_H, HG_SUB + t_tile, LANES), F32),
            pltpu.VMEM((t_tile, HG_V), BF16),
            pltpu.VMEM((t_tile, HG_QK), F32),
            pltpu.VMEM((t_tile, HG_V), F32),
            pltpu.VMEM((t_tile, cols1), BF16),
            pltpu.VMEM((t_tile, cols1), BF16),
            pltpu.VMEM((t_tile, cols2), F32),
            pltpu.VMEM((t_tile, cols2), F32),
            pltpu.VMEM((t_tile, HG_V), BF16),
        ],
        compiler_params=_params(2),
        name="hgrn2",
    )(*args)


def _rope_table(pos0, length):
    half = RET_DK // 2
    inv = ROPE_BASE ** (-np.arange(half, dtype=np.float64) / half)
    pos = pos0 + np.arange(length, dtype=np.float64)
    ang = pos[:, None] * inv[None, :]
    cos = np.cos(ang)
    sin = np.sin(ang)
    c2 = np.concatenate([cos, cos], axis=-1)
    s2 = np.concatenate([-sin, sin], axis=-1)
    ks = RET_DK ** -0.5
    return np.stack([c2, s2, c2 * ks, s2 * ks]).astype(np.float32)


def _ret_log_decay():
    return np.log1p(-(2.0 ** (-5.0 - np.arange(RET_H, dtype=np.float64))))


def _ret_decay_matrix(t_tile, chunk):
    lg = _ret_log_decay()
    n = np.arange(t_tile)
    cn = n // chunk
    diff = (n[:, None] - n[None, :]).astype(np.float64)
    same = cn[:, None] == cn[None, :]
    later = cn[:, None] > cn[None, :]
    expo = np.where(same, np.abs(diff), diff)
    dec = np.exp(lg[:, None, None] * expo[None])
    return np.where((same | later)[None], dec, 0.0).astype(np.float32)


def _lower_bounds(lb_param):
    c = jnp.cumsum(jax.nn.softmax(lb_param.astype(F32), axis=0), axis=0)
    return c - c[0]


def kernel(x_prompt, x_sample, state_ret, cache_swa_k, cache_swa_v, state_hgrn, norm_g, w_ff_in, w_ff_out,
           ret_w_in, ret_w_out, ret_gn_g, swa_w_in, swa_w_out, swa_sink, hg_w_in, hg_w_out, hg_gn_g, hg_lb):
    bf = lambda w: w.astype(BF16)
    w_ff_in, w_ff_out = bf(w_ff_in), bf(w_ff_out)
    ret_w_in, ret_w_out = bf(ret_w_in), bf(ret_w_out)
    swa_w_in, swa_w_out = bf(swa_w_in), bf(swa_w_out)
    hg_w_in, hg_w_out = bf(hg_w_in), bf(hg_w_out)
    lbs = _lower_bounds(hg_lb)

    bp, lp, _ = x_prompt.shape
    bs, ls, _ = x_sample.shape
    xp = x_prompt.reshape(bp * lp, D_MODEL)
    xs = x_sample.reshape(bs * ls, D_MODEL)
    rope_p, rope_s = _rope_table(0.0, lp), _rope_table(float(PAST_LEN), ls)
    dmat_p, dmat_s = _ret_decay_matrix(MIX_TOKENS, CHUNK), _ret_decay_matrix(ls, ls)
    lg_tab = np.broadcast_to(_ret_log_decay().astype(np.float32)[:, None, None], (RET_H, 1, RET_DV))
    ret_p, ret_s, k_p, v_p, k_s, v_s, hg_p, hg_s = [], [], [], [], [], [], [], []

    for li in range(DEPTH):
        kind, j = li % 3, li // 3
        xp = _ffn(xp, norm_g[li, 0:2], w_ff_in, w_ff_out, (li, 0))
        xs = _ffn(xs, norm_g[li, 0:2], w_ff_in, w_ff_out, (li, 0))
        xp3, xs3 = xp.reshape(bp, lp, D_MODEL), xs.reshape(bs, ls, D_MODEL)
        g2 = norm_g[li, 2:4]
        if kind == 0:
            gn = ret_gn_g[j][None, :]
            xp3, sp = _retention(xp3, g2, ret_w_in, ret_w_out, (j,), gn, rope_p, dmat_p, lg_tab, None,
                                 MIX_TOKENS, 1, RET_SUBTILES)
            xs3, ss = _retention(xs3, g2, ret_w_in, ret_w_out, (j,), gn, rope_s, dmat_s, lg_tab,
                                 state_ret, ls, DEC_STREAMS, 1)
            ret_p.append(sp)
            ret_s.append(ss)
        elif kind == 1:
            xp3, kn, vn = _swa_prompt(xp3, g2, swa_w_in, swa_w_out, (j,), swa_sink[j], SWA_TOKENS)
            k_p.append(kn.reshape(bp, WINDOW, SWA_HKV, SWA_DH))
            v_p.append(vn.reshape(bp, WINDOW, SWA_HKV, SWA_DH))
            kc = cache_swa_k[j].reshape(bs, WINDOW, SWA_KV)
            vc = cache_swa_v[j].reshape(bs, WINDOW, SWA_KV)
            xs3, kn, vn = _swa_decode(xs3, g2, swa_w_in, swa_w_out, (j,), swa_sink[j], kc, vc)
            k_s.append(kn.reshape(bs, ls, SWA_HKV, SWA_DH))
            v_s.append(vn.reshape(bs, ls, SWA_HKV, SWA_DH))
        else:
            gn, lb = hg_gn_g[j][None, :], lbs[li][None, :]
            xp3, sp = _hgrn(xp3, g2, hg_w_in, hg_w_out, (j,), gn, lb, None, MIX_TOKENS)
            xs3, ss = _hgrn(xs3, g2, hg_w_in, hg_w_out, (j,), gn, lb, state_hgrn[j], ls)
            hg_p.append(sp)
            hg_s.append(ss)
        xp, xs = xp3.reshape(bp * lp, D_MODEL), xs3.reshape(bs * ls, D_MODEL)
        xp = _ffn(xp, norm_g[li, 4:6], w_ff_in, w_ff_out, (li, 1))
        xs = _ffn(xs, norm_g[li, 4:6], w_ff_in, w_ff_out, (li, 1))

    return (xp.reshape(bp, lp, D_MODEL), xs.reshape(bs, ls, D_MODEL), jnp.stack(ret_p), jnp.stack(ret_s),
            jnp.stack(k_p), jnp.stack(v_p), jnp.stack(k_s), jnp.stack(v_s), jnp.stack(hg_p), jnp.stack(hg_s))
```

```python
import functools

import jax
import jax.numpy as jnp
import numpy as np
from jax import lax
from jax.experimental import pallas as pl
from jax.experimental.pallas import tpu as pltpu

F32 = jnp.float32
BF16 = jnp.bfloat16

D_MODEL = 1024
DEPTH = 4
CHUNK = 64
PAST_LEN = 4096
D_FF = 2816
EPS = 1e-6
RET_H, RET_DK, RET_DV = 8, 128, 256
RET_QK = RET_H * RET_DK
RET_V = RET_H * RET_DV
ROPE_BASE = 10000.0
WINDOW = 128
SWA_HQ, SWA_HKV, SWA_DH = 16, 2, 64
SWA_G = SWA_HQ // SWA_HKV
SWA_Q = SWA_HQ * SWA_DH
SWA_KV = SWA_HKV * SWA_DH
HG_H, HG_DK, HG_DV = 8, 128, 128
HG_QK = HG_H * HG_DK
HG_V = HG_H * HG_DV

VMEM_LIMIT_BYTES = 56 * 1024 * 1024
LANES = 128
SUBLANES = 8
MXU_DIM = 256
FFN_ROWS = 1024
MIX_TOKENS = 256
RET_SUBTILES = 2
SWA_TOKENS = 512
HG_CHUNK = 64
HG_SUB = SUBLANES
HG_BAND_ROWS = 32
DEC_STREAMS = 4
DEC_SWA_STREAMS = 16

NT_DIMS = (((1,), (1,)), ((), ()))
TN_DIMS = (((0,), (0,)), ((), ()))


def _rms(x, g):
    return x * lax.rsqrt(jnp.mean(x * x, axis=-1, keepdims=True) + EPS) * g


def _silu(x):
    return x * jax.nn.sigmoid(x)


def _dot(a, b):
    return jnp.dot(a, b, preferred_element_type=F32)


def _dot_nt(a, b):
    return lax.dot_general(a, b, NT_DIMS, preferred_element_type=F32)


def _dot_tn(a, b):
    return lax.dot_general(a, b, TN_DIMS, preferred_element_type=F32)


def _const_spec(shape):
    n = len(shape)
    return pl.BlockSpec(shape, lambda *_: (0,) * n, pipeline_mode=pl.Buffered(1))


def _layer_spec(stacked, idx):
    tail = stacked.shape[len(idx):]
    block = (None,) * len(idx) + tail
    return pl.BlockSpec(block, lambda *_: tuple(idx) + (0,) * len(tail), pipeline_mode=pl.Buffered(1))


def _params(n_axes):
    return pltpu.CompilerParams(
        dimension_semantics=("arbitrary",) * n_axes,
        vmem_limit_bytes=VMEM_LIMIT_BYTES)


def _ffn_kernel(x_ref, g_ref, win_ref, wout_ref, o_ref, act_ref):
    x = x_ref[...]
    h = _rms(x, g_ref[0:1, :]).astype(BF16)
    for c in range(D_FF // MXU_DIM):
        cols = slice(c * MXU_DIM, (c + 1) * MXU_DIM)
        ucols = slice(D_FF + c * MXU_DIM, D_FF + (c + 1) * MXU_DIM)
        a = _dot(h, win_ref[:, cols])
        u = _dot(h, win_ref[:, ucols])
        act_ref[:, cols] = (_silu(a) * u).astype(BF16)
    y = _dot(act_ref[...], wout_ref[...])
    o_ref[...] = x + 0.5 * _rms(y, g_ref[1:2, :])


def _ffn(x, g2, w_in, w_out, widx):
    m = x.shape[0]
    tm = min(FFN_ROWS, m)
    assert m % tm == 0
    row_spec = pl.BlockSpec((tm, D_MODEL), lambda i: (i, 0))
    return pl.pallas_call(
        _ffn_kernel,
        grid=(m // tm,),
        in_specs=[row_spec, _const_spec((2, D_MODEL)), _layer_spec(w_in, widx), _layer_spec(w_out, widx)],
        out_specs=row_spec,
        out_shape=jax.ShapeDtypeStruct(x.shape, F32),
        scratch_shapes=[pltpu.VMEM((tm, D_FF), BF16)],
        compiler_params=_params(1),
        name="ffn",
    )(x, g2, w_in, w_out)


def _ret_kernel(*refs, t_tile, n_streams, n_sub, has_state):
    if has_state:
        (x_ref, g_ref, win_ref, wout_ref, gn_ref, rope_ref, dmat_ref, lg_ref, s0_ref,
         o_ref, s_ref, q_s, k_s, v_s, gate_s, mix_s) = refs
    else:
        (x_ref, g_ref, win_ref, wout_ref, gn_ref, rope_ref, dmat_ref, lg_ref,
         o_ref, s_ref, q_s, k_s, v_s, gate_s, mix_s) = refs
        s0_ref = None
    t = pl.program_id(1)

    @pl.when(t == 0)
    def _():
        if has_state:
            s_ref[...] = s0_ref[...]
        else:
            s_ref[...] = jnp.zeros_like(s_ref)

    x = x_ref[...]
    h = _rms(x, g_ref[0:1, :]).astype(BF16)
    q_s[...] = _dot(h, win_ref[:, 0:RET_QK])
    k_s[...] = _dot(h, win_ref[:, RET_QK:2 * RET_QK])
    v_s[...] = _dot(h, win_ref[:, 2 * RET_QK:2 * RET_QK + RET_V]).astype(BF16)
    gate_s[...] = _dot(h, win_ref[:, 2 * RET_QK + RET_V:])

    row = lax.broadcasted_iota(jnp.int32, (t_tile, LANES), 0).astype(F32)
    for st, sub in [(a, b) for a in range(n_streams) for b in range(n_sub)]:
        rows = slice((st * n_sub + sub) * t_tile, (st * n_sub + sub + 1) * t_tile)
        pos = slice(sub * t_tile, (sub + 1) * t_tile)
        cq, sq, ck, sk = rope_ref[0, pos, :], rope_ref[1, pos, :], rope_ref[2, pos, :], rope_ref[3, pos, :]
        for hd in range(RET_H):
            qk = slice(hd * RET_DK, (hd + 1) * RET_DK)
            vv = slice(hd * RET_DV, (hd + 1) * RET_DV)
            lg = lg_ref[hd]
            lg_k = lg[:, :RET_DK]
            qh = q_s[rows, qk]
            kh = k_s[rows, qk]
            qr = qh * cq + pltpu.roll(qh, RET_DK // 2, 1) * sq
            kr = kh * ck + pltpu.roll(kh, RET_DK // 2, 1) * sk
            vh = v_s[rows, vv]
            p = (_dot_nt(qr.astype(BF16), kr.astype(BF16)) * dmat_ref[hd]).astype(BF16)
            q_cross = (qr * jnp.exp(lg_k * (row + 1.0))).astype(BF16)
            k_dec = (kr * jnp.exp(lg_k * (float(t_tile - 1) - row))).astype(BF16)
            s_old = s_ref[st, hd]
            o = _dot(p, vh) + _dot(q_cross, s_old.astype(BF16))
            s_ref[st, hd] = jnp.exp(lg * float(t_tile)) * s_old + _dot_tn(k_dec, vh)
            o = _rms(o, gn_ref[:, vv])
            mix_s[rows, vv] = (_silu(gate_s[rows, vv]) * o).astype(BF16)
    y = _dot(mix_s[...], wout_ref[...])
    o_ref[...] = x + _rms(y, g_ref[1:2, :])


def _retention(x, g2, w_in, w_out, widx, gn, rope_tab, dmat, lg_tab, s0, t_tile, n_streams, n_sub):
    b, l, _ = x.shape
    step_tokens = t_tile * n_sub
    assert b % n_streams == 0 and l % step_tokens == 0 and (n_streams == 1 or l == step_tokens)
    has_state = s0 is not None
    tiles = l // step_tokens
    rows = n_streams * step_tokens
    row_spec = pl.BlockSpec((rows, D_MODEL), lambda i, j: (i * tiles + j, 0))
    state_spec = pl.BlockSpec((n_streams, RET_H, RET_DK, RET_DV), lambda i, j: (i, 0, 0, 0))
    in_specs = [
        row_spec,
        _const_spec((2, D_MODEL)),
        _layer_spec(w_in, widx),
        _layer_spec(w_out, widx),
        _const_spec((1, RET_V)),
        pl.BlockSpec((4, step_tokens, LANES), lambda i, j: (0, j, 0)),
        _const_spec(dmat.shape),
        _const_spec(lg_tab.shape),
    ]
    args = [x.reshape(b * l, D_MODEL), g2, w_in, w_out, gn, rope_tab, dmat, lg_tab]
    if has_state:
        in_specs.append(pl.BlockSpec((None, n_streams, RET_H, RET_DK, RET_DV),
                                     lambda i, j: (widx[0], i, 0, 0, 0)))
        args.append(s0)
    y, s_new = pl.pallas_call(
        functools.partial(_ret_kernel, t_tile=t_tile, n_streams=n_streams, n_sub=n_sub, has_state=has_state),
        grid=(b // n_streams, tiles),
        in_specs=in_specs,
        out_specs=[row_spec, state_spec],
        out_shape=[jax.ShapeDtypeStruct((b * l, D_MODEL), F32),
                   jax.ShapeDtypeStruct((b, RET_H, RET_DK, RET_DV), F32)],
        scratch_shapes=[
            pltpu.VMEM((rows, RET_QK), F32),
            pltpu.VMEM((rows, RET_QK), F32),
            pltpu.VMEM((rows, RET_V), BF16),
            pltpu.VMEM((rows, RET_V), F32),
            pltpu.VMEM((rows, RET_V), BF16),
        ],
        compiler_params=_params(2),
        name="retention",
    )(*args)
    return y.reshape(b, l, D_MODEL), s_new


def _swa_expand(a):
    lane = lax.broadcasted_iota(jnp.int32, (1, LANES), 1)
    lo = lane < SWA_DH
    ar = pltpu.roll(a, SWA_DH, 1)
    zero = jnp.zeros_like(a)
    return (jnp.where(lo, a, zero).astype(BF16), jnp.where(lo, zero, ar).astype(BF16),
            jnp.where(lo, ar, zero).astype(BF16), jnp.where(lo, zero, a).astype(BF16))


def _swa_prompt_kernel(x_ref, g_ref, win_ref, wout_ref, sink_ref, bias_ref,
                       o_ref, ko_ref, vo_ref, q_s, kx_s, vt_s, vprev_s, att_s, sc_s, ex_s, *, t_tile):
    t = pl.program_id(1)
    half = 2 * CHUNK
    n_keys = WINDOW + half
    pairs = SWA_G // 2

    @pl.when(t == 0)
    def _():
        kx_s[:, 0:WINDOW, :] = jnp.zeros((4, WINDOW, LANES), BF16)
        vprev_s[...] = jnp.zeros((4, LANES, WINDOW), BF16)

    x = x_ref[...]
    h = _rms(x, g_ref[0:1, :]).astype(BF16)
    q_s[...] = _dot(h, win_ref[:, 0:SWA_Q]) * (SWA_DH ** -0.5)
    kv = _dot(h, win_ref[:, SWA_Q:])
    k_new = kv[:, 0:SWA_KV]
    v_new = kv[:, SWA_KV:]
    ko_ref[...] = k_new[t_tile - WINDOW:, :]
    vo_ref[...] = v_new[t_tile - WINDOW:, :]
    for i, a in enumerate(_swa_expand(k_new)):
        kx_s[i, WINDOW:, :] = a
    v_t = v_new.T
    ones = jnp.ones((SWA_DH, t_tile), F32)
    for grp in range(SWA_HKV):
        vg = v_t[grp * SWA_DH:(grp + 1) * SWA_DH, :]
        vt_s[2 * grp] = jnp.concatenate([vg, ones], axis=0).astype(BF16)
        vt_s[2 * grp + 1] = jnp.concatenate([ones, vg], axis=0).astype(BF16)

    first = jnp.where(t == 0, 1, 0)
    combos = [(hc, grp, pp, par) for hc in range(t_tile // half) for grp in range(SWA_HKV)
              for pp in range(pairs // 2) for par in range(2)]
    for i, (hc, grp, pp, par) in enumerate(combos):
        keys = slice(hc * half, hc * half + n_keys)
        qrows = slice(hc * half, (hc + 1) * half)
        bias = bias_ref[first] if hc == 0 else bias_ref[0]
        pa = grp * pairs + 2 * pp
        qst = jnp.concatenate(
            [q_s[qrows, pa * LANES:(pa + 1) * LANES],
             q_s[qrows, (pa + 1) * LANES:(pa + 2) * LANES]], axis=0).astype(BF16)
        sc_s[i] = _dot_nt(kx_s[grp * 2 + par, keys, :], qst) + bias
    sink_terms = []
    for i, (hc, grp, pp, par) in enumerate(combos):
        s = sc_s[i]
        snk = sink_ref[(grp * 2 + pp) * 2 + par]
        m = jnp.maximum(jnp.max(s, axis=0, keepdims=True), snk)
        ex_s[i] = jnp.exp(s - m).astype(BF16)
        sink_terms.append(jnp.exp(snk - m))
    for i, (hc, grp, pp, par) in enumerate(combos):
        idx = grp * 2 + par
        qrows = slice(hc * half, (hc + 1) * half)
        if hc == 0:
            v_keys = jnp.concatenate([vprev_s[idx], vt_s[idx, :, 0:half]], axis=1)
        else:
            v_keys = vt_s[idx, :, hc * half - WINDOW:(hc + 1) * half]
        ov = _dot(v_keys, ex_s[i])
        o_rows = slice(par * SWA_DH, (par + 1) * SWA_DH)
        d_row = slice((1 - par) * SWA_DH, (1 - par) * SWA_DH + 1)
        out = (ov[o_rows, :] * (1.0 / (ov[d_row, :] + sink_terms[i]))).astype(BF16)
        pa = grp * pairs + 2 * pp
        for a in range(2):
            base = (pa + a) * LANES + par * SWA_DH
            att_s[base:base + SWA_DH, qrows] = out[:, a * half:(a + 1) * half]

    kx_s[:, 0:WINDOW, :] = kx_s[:, t_tile:t_tile + WINDOW, :]
    vprev_s[...] = vt_s[:, :, t_tile - WINDOW:t_tile]

    y = _dot_tn(att_s[...], wout_ref[...])
    o_ref[...] = x + _rms(y, g_ref[1:2, :])


def _swa_prompt(x, g2, w_in, w_out, widx, sink, t_tile):
    b, l, _ = x.shape
    half = 2 * CHUNK
    n_keys = WINDOW + half
    s = sink.astype(F32).reshape(SWA_HKV, SWA_G // 4, 2, 2)
    s = jnp.transpose(s, (0, 1, 3, 2)).reshape(-1, 1, 2, 1)
    sink_tab = jnp.broadcast_to(s, (s.shape[0], 1, 2, half)).reshape(-1, 1, 2 * half)
    kc = np.arange(n_keys)[:, None] // CHUNK
    qc = (np.arange(2 * half)[None, :] % half) // CHUNK
    band = (kc >= qc) & (kc <= qc + WINDOW // CHUNK)
    started = np.arange(n_keys)[:, None] >= WINDOW
    bias = np.stack([np.where(band, 0.0, -np.inf),
                     np.where(band & started, 0.0, -np.inf)]).astype(np.float32)
    kv_out = pl.BlockSpec((None, WINDOW, SWA_KV), lambda i, j: (i, 0, 0))
    return pl.pallas_call(
        functools.partial(_swa_prompt_kernel, t_tile=t_tile),
        grid=(b, l // t_tile),
        in_specs=[
            pl.BlockSpec((None, t_tile, D_MODEL), lambda i, j: (i, j, 0)),
            _const_spec((2, D_MODEL)),
            _layer_spec(w_in, widx),
            _layer_spec(w_out, widx),
            _const_spec(sink_tab.shape),
            _const_spec(bias.shape),
        ],
        out_specs=[pl.BlockSpec((None, t_tile, D_MODEL), lambda i, j: (i, j, 0)), kv_out, kv_out],
        out_shape=[jax.ShapeDtypeStruct(x.shape, F32),
                   jax.ShapeDtypeStruct((b, WINDOW, SWA_KV), F32),
                   jax.ShapeDtypeStruct((b, WINDOW, SWA_KV), F32)],
        scratch_shapes=[
            pltpu.VMEM((t_tile, SWA_Q), F32),
            pltpu.VMEM((4, WINDOW + t_tile, LANES), BF16),
            pltpu.VMEM((4, LANES, t_tile), BF16),
            pltpu.VMEM((4, LANES, WINDOW), BF16),
            pltpu.VMEM((SWA_Q, t_tile), BF16),
            pltpu.VMEM((4 * t_tile // CHUNK, n_keys, 2 * half), F32),
            pltpu.VMEM((4 * t_tile // CHUNK, n_keys, 2 * half), BF16),
        ],
        compiler_params=_params(2),
        name="swa_prompt",
    )(x, g2, w_in, w_out, sink_tab, bias)


def _swa_decode_kernel(x_ref, g_ref, win_ref, wout_ref, sink_ref, kc_ref, vc_ref,
                       o_ref, ko_ref, vo_ref, q_s, kx_s, vx_s, att_s, *, t_tile, n_streams):
    pairs = SWA_G // 2
    x = x_ref[...]
    h = _rms(x, g_ref[0:1, :]).astype(BF16)
    q_s[...] = _dot(h, win_ref[:, 0:SWA_Q]) * (SWA_DH ** -0.5)
    kv = _dot(h, win_ref[:, SWA_Q:])
    ko_ref[...] = kv[:, 0:SWA_KV]
    vo_ref[...] = kv[:, SWA_KV:]

    for st in range(n_streams):
        rows = slice(st * t_tile, (st + 1) * t_tile)
        for i, a in enumerate(_swa_expand(kc_ref[st])):
            kx_s[st, i, 0:WINDOW, :] = a
        for i, a in enumerate(_swa_expand(vc_ref[st])):
            vx_s[st, i, 0:WINDOW, :] = a
        for i, a in enumerate(_swa_expand(kv[rows, 0:SWA_KV])):
            kx_s[st, i, WINDOW:, :] = a
        for i, a in enumerate(_swa_expand(kv[rows, SWA_KV:])):
            vx_s[st, i, WINDOW:, :] = a
        for grp in range(SWA_HKV):
            qst = jnp.concatenate(
                [q_s[rows, (grp * pairs + p) * LANES:(grp * pairs + p + 1) * LANES]
                 for p in range(pairs)], axis=0).astype(BF16)
            out = None
            for par in range(2):
                idx = grp * 2 + par
                s = _dot_nt(qst, kx_s[st, idx])
                snk = sink_ref[idx][:, 0:1]
                m = jnp.maximum(jnp.max(s, axis=-1, keepdims=True), snk)
                e = jnp.exp(s - m)
                den = jnp.sum(e, axis=-1, keepdims=True) + jnp.exp(snk - m)
                pv = _dot((e / den).astype(BF16), vx_s[st, idx])
                out = pv if out is None else out + pv
            for p in range(pairs):
                cols = slice((grp * pairs + p) * LANES, (grp * pairs + p + 1) * LANES)
                att_s[rows, cols] = out[p * t_tile:(p + 1) * t_tile, :].astype(BF16)

    y = _dot(att_s[...], wout_ref[...])
    o_ref[...] = x + _rms(y, g_ref[1:2, :])


def _swa_decode(x, g2, w_in, w_out, widx, sink, k_cache, v_cache, n_streams):
    b, l, _ = x.shape
    assert b % n_streams == 0
    rows = n_streams * l
    s = sink.astype(F32).reshape(SWA_HKV, SWA_G // 2, 2)
    s = jnp.transpose(s, (0, 2, 1)).reshape(SWA_HKV * 2, SWA_G // 2)
    s = jnp.repeat(s, l, axis=1)
    sink_tab = jnp.broadcast_to(s[:, :, None], s.shape + (LANES,))
    row_spec = pl.BlockSpec((rows, D_MODEL), lambda i: (i, 0))
    cache_spec = pl.BlockSpec((n_streams, WINDOW, SWA_KV), lambda i: (i, 0, 0))
    kv_out = pl.BlockSpec((rows, SWA_KV), lambda i: (i, 0))
    y, k_new, v_new = pl.pallas_call(
        functools.partial(_swa_decode_kernel, t_tile=l, n_streams=n_streams),
        grid=(b // n_streams,),
        in_specs=[
            row_spec,
            _const_spec((2, D_MODEL)),
            _layer_spec(w_in, widx),
            _layer_spec(w_out, widx),
            _const_spec(sink_tab.shape),
            cache_spec, cache_spec,
        ],
        out_specs=[row_spec, kv_out, kv_out],
        out_shape=[jax.ShapeDtypeStruct((b * l, D_MODEL), F32),
                   jax.ShapeDtypeStruct((b * l, SWA_KV), F32),
                   jax.ShapeDtypeStruct((b * l, SWA_KV), F32)],
        scratch_shapes=[
            pltpu.VMEM((rows, SWA_Q), F32),
            pltpu.VMEM((n_streams, 4, WINDOW + l, LANES), BF16),
            pltpu.VMEM((n_streams, 4, WINDOW + l, LANES), BF16),
            pltpu.VMEM((rows, SWA_Q), BF16),
        ],
        compiler_params=_params(1),
        name="swa_decode",
    )(x.reshape(b * l, D_MODEL), g2, w_in, w_out, sink_tab, k_cache, v_cache)
    return y.reshape(b, l, D_MODEL), k_new, v_new


def _split3(a):
    hi = a.astype(BF16)
    r = a - hi.astype(F32)
    mid = r.astype(BF16)
    low = (r - mid.astype(F32)).astype(BF16)
    return hi, mid, low


def _hgrn_kernel(*refs, t_tile, has_state):
    if has_state:
        (x_ref, g_ref, win_ref, wout_ref, gn_ref, lb_ref, tri_ref, mask_ref, s0_ref,
         o_ref, s_ref, st_s, q_s, kp_s, fp_s, vp_s, vb_s, b_s, gate_s,
         qc1_s, kc1_s, qc2_s, kc2_s, mix_s) = refs
    else:
        (x_ref, g_ref, win_ref, wout_ref, gn_ref, lb_ref, tri_ref, mask_ref,
         o_ref, s_ref, st_s, q_s, kp_s, fp_s, vp_s, vb_s, b_s, gate_s,
         qc1_s, kc1_s, qc2_s, kc2_s, mix_s) = refs
    t = pl.program_id(1)
    pad = HG_SUB
    chunk = min(t_tile, HG_CHUNK)
    n1 = t_tile // chunk
    nb2 = chunk // HG_SUB
    body = slice(pad, pad + t_tile)

    @pl.when(t == 0)
    def _():
        for hd in range(HG_H):
            if has_state:
                st_s[hd] = s0_ref[hd].T
            else:
                st_s[hd] = jnp.zeros((HG_DV, HG_DK), F32)
        zpad = jnp.zeros((HG_H, pad, LANES), F32)
        kp_s[:, 0:pad, :] = zpad
        fp_s[:, 0:pad, :] = zpad
        vp_s[:, 0:pad, :] = zpad
        qc1_s[...] = jnp.zeros_like(qc1_s)
        kc1_s[...] = jnp.zeros_like(kc1_s)
        qc2_s[...] = jnp.zeros_like(qc2_s)
        kc2_s[...] = jnp.zeros_like(kc2_s)

    x = x_ref[...]
    h = _rms(x, g_ref[0:1, :]).astype(BF16)
    lb = lb_ref[...]
    tri = tri_ref[...]
    hpc = MXU_DIM // LANES
    for c in range(HG_QK // MXU_DIM):
        cols = slice(c * MXU_DIM, (c + 1) * MXU_DIM)

        def proj(part, cols=cols):
            return _dot(h, win_ref[:, part * HG_QK + cols.start:part * HG_QK + cols.stop])

        lbc = lb[:, cols]
        q_s[:, cols] = _silu(proj(0))
        f = proj(1)
        fgate = lbc + (1.0 - lbc) * jax.nn.sigmoid(f)
        k_all = (1.0 - lbc) * jax.nn.sigmoid(-f)
        v = proj(2)
        vb_s[:, cols] = v.astype(BF16)
        gate_s[:, cols] = proj(3)
        for i in range(hpc):
            sub = slice(i * LANES, (i + 1) * LANES)
            fp_s[c * hpc + i, body, :] = fgate[:, sub]
            kp_s[c * hpc + i, body, :] = k_all[:, sub]
            vp_s[c * hpc + i, body, :] = v[:, sub]
        hi, mid, low = _split3(jnp.log(fgate))
        b_s[:, cols] = _dot(tri, hi) + _dot(tri, mid) + _dot(tri, low)

    for hd in range(HG_H):
        ln = slice(hd * LANES, (hd + 1) * LANES)
        q = q_s[:, ln]
        k = kp_s[hd, body, :]
        bc = b_s[:, ln]
        vb = vb_s[:, ln]

        for j in range(n1 - 1):
            blk = slice(j * chunk, (j + 1) * chunk)
            after = slice((j + 1) * chunk, t_tile)
            cols = slice(j * LANES, (j + 1) * LANES)
            e_j = bc[(j + 1) * chunk - 1:(j + 1) * chunk, :]
            kc1_s[blk, cols] = (k[blk] * jnp.exp(e_j - bc[blk])).astype(BF16)
            qc1_s[after, cols] = (q[after] * jnp.exp(bc[after] - e_j)).astype(BF16)
        for c in range(n1):
            for j in range(nb2 - 1):
                r0 = c * chunk + j * HG_SUB
                blk = slice(r0, r0 + HG_SUB)
                after = slice(r0 + HG_SUB, (c + 1) * chunk)
                cols = slice(j * LANES, (j + 1) * LANES)
                e_j = bc[r0 + HG_SUB - 1:r0 + HG_SUB, :]
                kc2_s[blk, cols] = k[blk] * jnp.exp(e_j - bc[blk])
                qc2_s[after, cols] = q[after] * jnp.exp(bc[after] - e_j)

        p = _dot_nt(qc2_s[...].astype(BF16), kc2_s[...].astype(BF16)) * mask_ref[1]
        if n1 > 1:
            p = p + _dot_nt(qc1_s[...], kc1_s[...]) * mask_ref[0]
        st_old = st_s[hd]
        o = _dot(p.astype(BF16), vb) + _dot_nt((q * jnp.exp(bc)).astype(BF16), st_old.astype(BF16))

        rows = min(t_tile, HG_BAND_ROWS)
        band = []
        for r0 in range(0, t_tile, rows):
            qb = q_s[r0:r0 + rows, ln]
            acc = None
            prod = None
            for d in range(HG_SUB):
                sh = slice(pad - d + r0, pad - d + r0 + rows)
                w = qb * kp_s[hd, sh, :]
                if d >= 1:
                    fsh = fp_s[hd, pad - d + 1 + r0:pad - d + 1 + r0 + rows, :]
                    prod = fsh if d == 1 else prod * fsh
                    w = w * prod
                term = jnp.sum(w, axis=-1, keepdims=True) * vp_s[hd, sh, :]
                acc = term if acc is None else acc + term
            band.append(acc)
        o = o + jnp.concatenate(band, axis=0)

        b_last = bc[t_tile - 1:t_tile, :]
        k_end = (k * jnp.exp(b_last - bc)).astype(BF16)
        st_new = st_old * jnp.exp(b_last) + _dot_tn(vb, k_end)
        st_s[hd] = st_new
        s_ref[hd] = st_new.T

        o = _rms(o, gn_ref[:, ln])
        mix_s[:, ln] = (_silu(gate_s[:, ln]) * o).astype(BF16)

    y = _dot(mix_s[...], wout_ref[...])
    o_ref[...] = x + _rms(y, g_ref[1:2, :])


def _hgrn(x, g2, w_in, w_out, widx, gn, lb, s0, t_tile):
    b, l, _ = x.shape
    has_state = s0 is not None
    chunk = min(t_tile, HG_CHUNK)
    n1 = t_tile // chunk
    nb2 = chunk // HG_SUB
    tri = jnp.asarray(np.tril(np.ones((t_tile, t_tile), np.float32)), dtype=BF16)
    n = np.arange(t_tile)
    far = (n[:, None] - n[None, :]) >= HG_SUB
    same = (n[:, None] // chunk) == (n[None, :] // chunk)
    masks = np.stack([far, far & same]).astype(np.float32)
    state_spec = pl.BlockSpec((None, HG_H, HG_DK, HG_DV), lambda i, j: (i, 0, 0, 0))
    in_specs = [
        pl.BlockSpec((None, t_tile, D_MODEL), lambda i, j: (i, j, 0)),
        _const_spec((2, D_MODEL)),
        _layer_spec(w_in, widx),
        _layer_spec(w_out, widx),
        _const_spec((1, HG_V)),
        _const_spec((1, HG_QK)),
        _const_spec(tri.shape),
        _const_spec(masks.shape),
    ]
    args = [x, g2, w_in, w_out, gn, lb, tri, masks]
    if has_state:
        in_specs.append(state_spec)
        args.append(s0)
    cols1 = max(n1 - 1, 1) * LANES
    cols2 = (nb2 - 1) * LANES
    return pl.pallas_call(
        functools.partial(_hgrn_kernel, t_tile=t_tile, has_state=has_state),
        grid=(b, l // t_tile),
        in_specs=in_specs,
        out_specs=[pl.BlockSpec((None, t_tile, D_MODEL), lambda i, j: (i, j, 0)), state_spec],
        out_shape=[jax.ShapeDtypeStruct(x.shape, F32),
                   jax.ShapeDtypeStruct((b, HG_H, HG_DK, HG_DV), F32)],
        scratch_shapes=[
            pltpu.VMEM((HG_H, HG_DV, HG_DK), F32),
            pltpu.VMEM((t_tile, HG_QK), F32),
            pltpu.VMEM((HG_H, HG_SUB + t_tile, LANES), F32),
            pltpu.VMEM((HG_H, HG_SUB + t_tile, LANES), F32),
            pltpu.VMEM((HG_H, HG_SUB + t_tile, LANES), F32),
            pltpu.VMEM((t_tile, HG_V), BF16),
            pltpu.VMEM((t_tile, HG_QK), F32),
            pltpu.VMEM((t_tile, HG_V), F32),
            pltpu.VMEM((t_tile, cols1), BF16),
            pltpu.VMEM((t_tile, cols1), BF16),
            pltpu.VMEM((t_tile, cols2), F32),
            pltpu.VMEM((t_tile, cols2), F32),
            pltpu.VMEM((t_tile, HG_V), BF16),
        ],
        compiler_params=_params(2),
        name="hgrn2",
    )(*args)


def _rope_table(pos0, length):
    half = RET_DK // 2
    inv = ROPE_BASE ** (-np.arange(half, dtype=np.float64) / half)
    pos = pos0 + np.arange(length, dtype=np.float64)
    ang = pos[:, None] * inv[None, :]
    cos = np.cos(ang)
    sin = np.sin(ang)
    c2 = np.concatenate([cos, cos], axis=-1)
    s2 = np.concatenate([-sin, sin], axis=-1)
    ks = RET_DK ** -0.5
    return np.stack([c2, s2, c2 * ks, s2 * ks]).astype(np.float32)


def _ret_log_decay():
    return np.log1p(-(2.0 ** (-5.0 - np.arange(RET_H, dtype=np.float64))))


def _ret_decay_matrix(t_tile, chunk):
    lg = _ret_log_decay()
    n = np.arange(t_tile)
    cn = n // chunk
    diff = (n[:, None] - n[None, :]).astype(np.float64)
    same = cn[:, None] == cn[None, :]
    later = cn[:, None] > cn[None, :]
    expo = np.where(same, np.abs(diff), diff)
    dec = np.exp(lg[:, None, None] * expo[None])
    return np.where((same | later)[None], dec, 0.0).astype(np.float32)


def _lower_bounds(lb_param):
    c = jnp.cumsum(jax.nn.softmax(lb_param.astype(F32), axis=0), axis=0)
    return c - c[0]


def kernel(x_prompt, x_sample, state_ret, cache_swa_k, cache_swa_v, state_hgrn, norm_g, w_ff_in, w_ff_out,
           ret_w_in, ret_w_out, ret_gn_g, swa_w_in, swa_w_out, swa_sink, hg_w_in, hg_w_out, hg_gn_g, hg_lb):
    bf = lambda w: w.astype(BF16)
    w_ff_in, w_ff_out = bf(w_ff_in), bf(w_ff_out)
    ret_w_in, ret_w_out = bf(ret_w_in), bf(ret_w_out)
    swa_w_in, swa_w_out = bf(swa_w_in), bf(swa_w_out)
    hg_w_in, hg_w_out = bf(hg_w_in), bf(hg_w_out)
    lbs = _lower_bounds(hg_lb)

    bp, lp, _ = x_prompt.shape
    bs, ls, _ = x_sample.shape
    xp = x_prompt.reshape(bp * lp, D_MODEL)
    xs = x_sample.reshape(bs * ls, D_MODEL)
    rope_p, rope_s = _rope_table(0.0, lp), _rope_table(float(PAST_LEN), ls)
    dmat_p, dmat_s = _ret_decay_matrix(MIX_TOKENS, CHUNK), _ret_decay_matrix(ls, ls)
    lg_tab = np.broadcast_to(_ret_log_decay().astype(np.float32)[:, None, None], (RET_H, 1, RET_DV))
    ret_p, ret_s, k_p, v_p, k_s, v_s, hg_p, hg_s = [], [], [], [], [], [], [], []

    for li in range(DEPTH):
        kind, j = li % 3, li // 3
        xp = _ffn(xp, norm_g[li, 0:2], w_ff_in, w_ff_out, (li, 0))
        xs = _ffn(xs, norm_g[li, 0:2], w_ff_in, w_ff_out, (li, 0))
        xp3, xs3 = xp.reshape(bp, lp, D_MODEL), xs.reshape(bs, ls, D_MODEL)
        g2 = norm_g[li, 2:4]
        if kind == 0:
            gn = ret_gn_g[j][None, :]
            xp3, sp = _retention(xp3, g2, ret_w_in, ret_w_out, (j,), gn, rope_p, dmat_p, lg_tab, None,
                                 MIX_TOKENS, 1, RET_SUBTILES)
            xs3, ss = _retention(xs3, g2, ret_w_in, ret_w_out, (j,), gn, rope_s, dmat_s, lg_tab,
                                 state_ret, ls, DEC_STREAMS, 1)
            ret_p.append(sp)
            ret_s.append(ss)
        elif kind == 1:
            xp3, kn, vn = _swa_prompt(xp3, g2, swa_w_in, swa_w_out, (j,), swa_sink[j], SWA_TOKENS)
            k_p.append(kn.reshape(bp, WINDOW, SWA_HKV, SWA_DH))
            v_p.append(vn.reshape(bp, WINDOW, SWA_HKV, SWA_DH))
            kc = cache_swa_k[j].reshape(bs, WINDOW, SWA_KV)
            vc = cache_swa_v[j].reshape(bs, WINDOW, SWA_KV)
            xs3, kn, vn = _swa_decode(xs3, g2, swa_w_in, swa_w_out, (j,), swa_sink[j], kc, vc, DEC_SWA_STREAMS)
            k_s.append(kn.reshape(bs, ls, SWA_HKV, SWA_DH))
            v_s.append(vn.reshape(bs, ls, SWA_HKV, SWA_DH))
        else:
            gn, lb = hg_gn_g[j][None, :], lbs[li][None, :]
            xp3, sp = _hgrn(xp3, g2, hg_w_in, hg_w_out, (j,), gn, lb, None, MIX_TOKENS)
            xs3, ss = _hgrn(xs3, g2, hg_w_in, hg_w_out, (j,), gn, lb, state_hgrn[j], ls)
            hg_p.append(sp)
            hg_s.append(ss)
        xp, xs = xp3.reshape(bp * lp, D_MODEL), xs3.reshape(bs * ls, D_MODEL)
        xp = _ffn(xp, norm_g[li, 4:6], w_ff_in, w_ff_out, (li, 1))
        xs = _ffn(xs, norm_g[li, 4:6], w_ff_in, w_ff_out, (li, 1))

    return (xp.reshape(bp, lp, D_MODEL), xs.reshape(bs, ls, D_MODEL), jnp.stack(ret_p), jnp.stack(ret_s),
            jnp.stack(k_p), jnp.stack(v_p), jnp.stack(k_s), jnp.stack(v_s), jnp.stack(hg_p), jnp.stack(hg_s))
```

```python
import functools

import jax
import jax.numpy as jnp
import numpy as np
from jax import lax
from jax.experimental import pallas as pl
from jax.experimental.pallas import tpu as pltpu

F32 = jnp.float32
BF16 = jnp.bfloat16

D_MODEL = 1024
DEPTH = 4
CHUNK = 64
PAST_LEN = 4096
D_FF = 2816
EPS = 1e-6
RET_H, RET_DK, RET_DV = 8, 128, 256
RET_QK = RET_H * RET_DK
RET_V = RET_H * RET_DV
ROPE_BASE = 10000.0
WINDOW = 128
SWA_HQ, SWA_HKV, SWA_DH = 16, 2, 64
SWA_G = SWA_HQ // SWA_HKV
SWA_Q = SWA_HQ * SWA_DH
SWA_KV = SWA_HKV * SWA_DH
HG_H, HG_DK, HG_DV = 8, 128, 128
HG_QK = HG_H * HG_DK
HG_V = HG_H * HG_DV

VMEM_LIMIT_BYTES = 56 * 1024 * 1024
LANES = 128
SUBLANES = 8
MXU_DIM = 256
FFN_ROWS = 1024
MIX_TOKENS = 256
RET_SUBTILES = 2
SWA_TOKENS = 512
HG_CHUNK = 64
HG_SUB = SUBLANES
HG_BAND_ROWS = 32
DEC_STREAMS = 4
DEC_SWA_STREAMS = 16

NT_DIMS = (((1,), (1,)), ((), ()))
TN_DIMS = (((0,), (0,)), ((), ()))


def _rms(x, g):
    return x * lax.rsqrt(jnp.mean(x * x, axis=-1, keepdims=True) + EPS) * g


def _silu(x):
    return x * jax.nn.sigmoid(x)


def _dot(a, b):
    return jnp.dot(a, b, preferred_element_type=F32)


def _dot_nt(a, b):
    return lax.dot_general(a, b, NT_DIMS, preferred_element_type=F32)


def _dot_tn(a, b):
    return lax.dot_general(a, b, TN_DIMS, preferred_element_type=F32)


def _const_spec(shape):
    n = len(shape)
    return pl.BlockSpec(shape, lambda *_: (0,) * n, pipeline_mode=pl.Buffered(1))


def _layer_spec(stacked, idx):
    tail = stacked.shape[len(idx):]
    block = (None,) * len(idx) + tail
    return pl.BlockSpec(block, lambda *_: tuple(idx) + (0,) * len(tail), pipeline_mode=pl.Buffered(1))


def _params(n_axes):
    return pltpu.CompilerParams(
        dimension_semantics=("arbitrary",) * n_axes,
        vmem_limit_bytes=VMEM_LIMIT_BYTES)


def _ffn_kernel(x_ref, g_ref, win_ref, wout_ref, o_ref, act_ref):
    x = x_ref[...]
    h = _rms(x, g_ref[0:1, :]).astype(BF16)
    for c in range(D_FF // MXU_DIM):
        cols = slice(c * MXU_DIM, (c + 1) * MXU_DIM)
        ucols = slice(D_FF + c * MXU_DIM, D_FF + (c + 1) * MXU_DIM)
        a = _dot(h, win_ref[:, cols])
        u = _dot(h, win_ref[:, ucols])
        act_ref[:, cols] = (_silu(a) * u).astype(BF16)
    y = _dot(act_ref[...], wout_ref[...])
    o_ref[...] = x + 0.5 * _rms(y, g_ref[1:2, :])


def _ffn(x, g2, w_in, w_out, widx):
    m = x.shape[0]
    tm = min(FFN_ROWS, m)
    assert m % tm == 0
    row_spec = pl.BlockSpec((tm, D_MODEL), lambda i: (i, 0))
    return pl.pallas_call(
        _ffn_kernel,
        grid=(m // tm,),
        in_specs=[row_spec, _const_spec((2, D_MODEL)), _layer_spec(w_in, widx), _layer_spec(w_out, widx)],
        out_specs=row_spec,
        out_shape=jax.ShapeDtypeStruct(x.shape, F32),
        scratch_shapes=[pltpu.VMEM((tm, D_FF), BF16)],
        compiler_params=_params(1),
        name="ffn",
    )(x, g2, w_in, w_out)


def _ret_kernel(*refs, t_tile, n_streams, n_sub, has_state, slab, has_stack):
    if has_stack:
        refs = refs[1:]
    if has_state:
        (x_ref, g_ref, win_ref, wout_ref, gn_ref, rope_ref, dmat_ref, lg_ref, s0_ref,
         o_ref, s_ref, q_s, k_s, v_s, gate_s, mix_s) = refs
    else:
        (x_ref, g_ref, win_ref, wout_ref, gn_ref, rope_ref, dmat_ref, lg_ref,
         o_ref, s_ref, q_s, k_s, v_s, gate_s, mix_s) = refs
        s0_ref = None
    t = pl.program_id(1)

    @pl.when(t == 0)
    def _():
        s_ref[...] = jnp.zeros_like(s_ref)
        if has_state:
            s_ref[slab] = s0_ref[...]

    x = x_ref[...]
    h = _rms(x, g_ref[0:1, :]).astype(BF16)
    q_s[...] = _dot(h, win_ref[:, 0:RET_QK])
    k_s[...] = _dot(h, win_ref[:, RET_QK:2 * RET_QK])
    v_s[...] = _dot(h, win_ref[:, 2 * RET_QK:2 * RET_QK + RET_V]).astype(BF16)
    gate_s[...] = _dot(h, win_ref[:, 2 * RET_QK + RET_V:])

    row = lax.broadcasted_iota(jnp.int32, (t_tile, LANES), 0).astype(F32)
    for st, sub in [(a, b) for a in range(n_streams) for b in range(n_sub)]:
        rows = slice((st * n_sub + sub) * t_tile, (st * n_sub + sub + 1) * t_tile)
        pos = slice(sub * t_tile, (sub + 1) * t_tile)
        cq, sq, ck, sk = rope_ref[0, pos, :], rope_ref[1, pos, :], rope_ref[2, pos, :], rope_ref[3, pos, :]
        for hd in range(RET_H):
            qk = slice(hd * RET_DK, (hd + 1) * RET_DK)
            vv = slice(hd * RET_DV, (hd + 1) * RET_DV)
            lg = lg_ref[hd]
            lg_k = lg[:, :RET_DK]
            qh = q_s[rows, qk]
            kh = k_s[rows, qk]
            qr = qh * cq + pltpu.roll(qh, RET_DK // 2, 1) * sq
            kr = kh * ck + pltpu.roll(kh, RET_DK // 2, 1) * sk
            vh = v_s[rows, vv]
            p = (_dot_nt(qr.astype(BF16), kr.astype(BF16)) * dmat_ref[hd]).astype(BF16)
            q_cross = (qr * jnp.exp(lg_k * (row + 1.0))).astype(BF16)
            k_dec = (kr * jnp.exp(lg_k * (float(t_tile - 1) - row))).astype(BF16)
            s_old = s_ref[slab, st, hd]
            o = _dot(p, vh) + _dot(q_cross, s_old.astype(BF16))
            s_ref[slab, st, hd] = jnp.exp(lg * float(t_tile)) * s_old + _dot_tn(k_dec, vh)
            o = _rms(o, gn_ref[:, vv])
            mix_s[rows, vv] = (_silu(gate_s[rows, vv]) * o).astype(BF16)
    y = _dot(mix_s[...], wout_ref[...])
    o_ref[...] = x + _rms(y, g_ref[1:2, :])


def _retention(x, g2, w_in, w_out, widx, gn, rope_tab, dmat, lg_tab, s0, t_tile, n_streams, n_sub, stack):
    b, l, _ = x.shape
    step_tokens = t_tile * n_sub
    assert b % n_streams == 0 and l % step_tokens == 0 and (n_streams == 1 or l == step_tokens)
    has_state = s0 is not None
    has_stack = stack is not None
    layer, n_layers = widx[0], w_in.shape[0]
    tiles = l // step_tokens
    rows = n_streams * step_tokens
    row_spec = pl.BlockSpec((rows, D_MODEL), lambda i, j: (i * tiles + j, 0))
    slabs = 1 if has_stack else n_layers
    slab_block = layer if has_stack else 0
    state_spec = pl.BlockSpec((slabs, n_streams, RET_H, RET_DK, RET_DV),
                              lambda i, j: (slab_block, i, 0, 0, 0))
    in_specs = [
        row_spec,
        _const_spec((2, D_MODEL)),
        _layer_spec(w_in, widx),
        _layer_spec(w_out, widx),
        _const_spec((1, RET_V)),
        pl.BlockSpec((4, step_tokens, LANES), lambda i, j: (0, j, 0)),
        _const_spec(dmat.shape),
        _const_spec(lg_tab.shape),
    ]
    args = [x.reshape(b * l, D_MODEL), g2, w_in, w_out, gn, rope_tab, dmat, lg_tab]
    if has_state:
        in_specs.append(pl.BlockSpec((None, n_streams, RET_H, RET_DK, RET_DV),
                                     lambda i, j: (widx[0], i, 0, 0, 0)))
        args.append(s0)
    if has_stack:
        in_specs.insert(0, pl.BlockSpec(memory_space=pl.ANY))
        args.insert(0, stack)
    y, s_new = pl.pallas_call(
        functools.partial(_ret_kernel, t_tile=t_tile, n_streams=n_streams, n_sub=n_sub, has_state=has_state,
                          slab=0 if has_stack else layer, has_stack=has_stack),
        grid=(b // n_streams, tiles),
        in_specs=in_specs,
        out_specs=[row_spec, state_spec],
        out_shape=[jax.ShapeDtypeStruct((b * l, D_MODEL), F32),
                   jax.ShapeDtypeStruct((n_layers, b, RET_H, RET_DK, RET_DV), F32)],
        input_output_aliases={0: 1} if has_stack else {},
        scratch_shapes=[
            pltpu.VMEM((rows, RET_QK), F32),
            pltpu.VMEM((rows, RET_QK), F32),
            pltpu.VMEM((rows, RET_V), BF16),
            pltpu.VMEM((rows, RET_V), F32),
            pltpu.VMEM((rows, RET_V), BF16),
        ],
        compiler_params=_params(2),
        name="retention",
    )(*args)
    return y.reshape(b, l, D_MODEL), s_new


def _swa_expand(a):
    lane = lax.broadcasted_iota(jnp.int32, (1, LANES), 1)
    lo = lane < SWA_DH
    ar = pltpu.roll(a, SWA_DH, 1)
    zero = jnp.zeros_like(a)
    return (jnp.where(lo, a, zero).astype(BF16), jnp.where(lo, zero, ar).astype(BF16),
            jnp.where(lo, ar, zero).astype(BF16), jnp.where(lo, zero, a).astype(BF16))


def _swa_prompt_kernel(x_ref, g_ref, win_ref, wout_ref, sink_ref, bias_ref,
                       o_ref, ko_ref, vo_ref, q_s, kx_s, vt_s, vprev_s, att_s, sc_s, ex_s, *, t_tile):
    t = pl.program_id(1)
    half = 2 * CHUNK
    n_keys = WINDOW + half
    pairs = SWA_G // 2

    @pl.when(t == 0)
    def _():
        kx_s[:, 0:WINDOW, :] = jnp.zeros((4, WINDOW, LANES), BF16)
        vprev_s[...] = jnp.zeros((4, LANES, WINDOW), BF16)

    x = x_ref[...]
    h = _rms(x, g_ref[0:1, :]).astype(BF16)
    q_s[...] = _dot(h, win_ref[:, 0:SWA_Q]) * (SWA_DH ** -0.5)
    kv = _dot(h, win_ref[:, SWA_Q:])
    k_new = kv[:, 0:SWA_KV]
    v_new = kv[:, SWA_KV:]
    ko_ref[...] = k_new[t_tile - WINDOW:, :]
    vo_ref[...] = v_new[t_tile - WINDOW:, :]
    for i, a in enumerate(_swa_expand(k_new)):
        kx_s[i, WINDOW:, :] = a
    v_t = v_new.T
    ones = jnp.ones((SWA_DH, t_tile), F32)
    for grp in range(SWA_HKV):
        vg = v_t[grp * SWA_DH:(grp + 1) * SWA_DH, :]
        vt_s[2 * grp] = jnp.concatenate([vg, ones], axis=0).astype(BF16)
        vt_s[2 * grp + 1] = jnp.concatenate([ones, vg], axis=0).astype(BF16)

    first = jnp.where(t == 0, 1, 0)
    combos = [(hc, grp, pp, par) for hc in range(t_tile // half) for grp in range(SWA_HKV)
              for pp in range(pairs // 2) for par in range(2)]
    for i, (hc, grp, pp, par) in enumerate(combos):
        keys = slice(hc * half, hc * half + n_keys)
        qrows = slice(hc * half, (hc + 1) * half)
        bias = bias_ref[first] if hc == 0 else bias_ref[0]
        pa = grp * pairs + 2 * pp
        qst = jnp.concatenate(
            [q_s[qrows, pa * LANES:(pa + 1) * LANES],
             q_s[qrows, (pa + 1) * LANES:(pa + 2) * LANES]], axis=0).astype(BF16)
        sc_s[i] = _dot_nt(kx_s[grp * 2 + par, keys, :], qst) + bias
    sink_terms = []
    for i, (hc, grp, pp, par) in enumerate(combos):
        s = sc_s[i]
        snk = sink_ref[(grp * 2 + pp) * 2 + par]
        m = jnp.maximum(jnp.max(s, axis=0, keepdims=True), snk)
        ex_s[i] = jnp.exp(s - m).astype(BF16)
        sink_terms.append(jnp.exp(snk - m))
    for i, (hc, grp, pp, par) in enumerate(combos):
        idx = grp * 2 + par
        qrows = slice(hc * half, (hc + 1) * half)
        if hc == 0:
            v_keys = jnp.concatenate([vprev_s[idx], vt_s[idx, :, 0:half]], axis=1)
        else:
            v_keys = vt_s[idx, :, hc * half - WINDOW:(hc + 1) * half]
        ov = _dot(v_keys, ex_s[i])
        o_rows = slice(par * SWA_DH, (par + 1) * SWA_DH)
        d_row = slice((1 - par) * SWA_DH, (1 - par) * SWA_DH + 1)
        out = (ov[o_rows, :] * (1.0 / (ov[d_row, :] + sink_terms[i]))).astype(BF16)
        pa = grp * pairs + 2 * pp
        for a in range(2):
            base = (pa + a) * LANES + par * SWA_DH
            att_s[base:base + SWA_DH, qrows] = out[:, a * half:(a + 1) * half]

    kx_s[:, 0:WINDOW, :] = kx_s[:, t_tile:t_tile + WINDOW, :]
    vprev_s[...] = vt_s[:, :, t_tile - WINDOW:t_tile]

    y = _dot_tn(att_s[...], wout_ref[...])
    o_ref[...] = x + _rms(y, g_ref[1:2, :])


def _swa_prompt(x, g2, w_in, w_out, widx, sink, t_tile):
    b, l, _ = x.shape
    half = 2 * CHUNK
    n_keys = WINDOW + half
    s = sink.astype(F32).reshape(SWA_HKV, SWA_G // 4, 2, 2)
    s = jnp.transpose(s, (0, 1, 3, 2)).reshape(-1, 1, 2, 1)
    sink_tab = jnp.broadcast_to(s, (s.shape[0], 1, 2, half)).reshape(-1, 1, 2 * half)
    kc = np.arange(n_keys)[:, None] // CHUNK
    qc = (np.arange(2 * half)[None, :] % half) // CHUNK
    band = (kc >= qc) & (kc <= qc + WINDOW // CHUNK)
    started = np.arange(n_keys)[:, None] >= WINDOW
    bias = np.stack([np.where(band, 0.0, -np.inf),
                     np.where(band & started, 0.0, -np.inf)]).astype(np.float32)
    kv_out = pl.BlockSpec((None, WINDOW, SWA_KV), lambda i, j: (i, 0, 0))
    return pl.pallas_call(
        functools.partial(_swa_prompt_kernel, t_tile=t_tile),
        grid=(b, l // t_tile),
        in_specs=[
            pl.BlockSpec((None, t_tile, D_MODEL), lambda i, j: (i, j, 0)),
            _const_spec((2, D_MODEL)),
            _layer_spec(w_in, widx),
            _layer_spec(w_out, widx),
            _const_spec(sink_tab.shape),
            _const_spec(bias.shape),
        ],
        out_specs=[pl.BlockSpec((None, t_tile, D_MODEL), lambda i, j: (i, j, 0)), kv_out, kv_out],
        out_shape=[jax.ShapeDtypeStruct(x.shape, F32),
                   jax.ShapeDtypeStruct((b, WINDOW, SWA_KV), F32),
                   jax.ShapeDtypeStruct((b, WINDOW, SWA_KV), F32)],
        scratch_shapes=[
            pltpu.VMEM((t_tile, SWA_Q), F32),
            pltpu.VMEM((4, WINDOW + t_tile, LANES), BF16),
            pltpu.VMEM((4, LANES, t_tile), BF16),
            pltpu.VMEM((4, LANES, WINDOW), BF16),
            pltpu.VMEM((SWA_Q, t_tile), BF16),
            pltpu.VMEM((4 * t_tile // CHUNK, n_keys, 2 * half), F32),
            pltpu.VMEM((4 * t_tile // CHUNK, n_keys, 2 * half), BF16),
        ],
        compiler_params=_params(2),
        name="swa_prompt",
    )(x, g2, w_in, w_out, sink_tab, bias)


def _swa_decode_kernel(x_ref, g_ref, win_ref, wout_ref, sink_ref, kc_ref, vc_ref,
                       o_ref, ko_ref, vo_ref, q_s, kx_s, vx_s, att_s, *, t_tile, n_streams):
    pairs = SWA_G // 2
    x = x_ref[...]
    h = _rms(x, g_ref[0:1, :]).astype(BF16)
    q_s[...] = _dot(h, win_ref[:, 0:SWA_Q]) * (SWA_DH ** -0.5)
    kv = _dot(h, win_ref[:, SWA_Q:])
    ko_ref[...] = kv[:, 0:SWA_KV]
    vo_ref[...] = kv[:, SWA_KV:]

    for st in range(n_streams):
        rows = slice(st * t_tile, (st + 1) * t_tile)
        for i, a in enumerate(_swa_expand(kc_ref[st])):
            kx_s[st, i, 0:WINDOW, :] = a
        for i, a in enumerate(_swa_expand(vc_ref[st])):
            vx_s[st, i, 0:WINDOW, :] = a
        for i, a in enumerate(_swa_expand(kv[rows, 0:SWA_KV])):
            kx_s[st, i, WINDOW:, :] = a
        for i, a in enumerate(_swa_expand(kv[rows, SWA_KV:])):
            vx_s[st, i, WINDOW:, :] = a
        for grp in range(SWA_HKV):
            qst = jnp.concatenate(
                [q_s[rows, (grp * pairs + p) * LANES:(grp * pairs + p + 1) * LANES]
                 for p in range(pairs)], axis=0).astype(BF16)
            out = None
            for par in range(2):
                idx = grp * 2 + par
                s = _dot_nt(qst, kx_s[st, idx])
                snk = sink_ref[idx][:, 0:1]
                m = jnp.maximum(jnp.max(s, axis=-1, keepdims=True), snk)
                e = jnp.exp(s - m)
                den = jnp.sum(e, axis=-1, keepdims=True) + jnp.exp(snk - m)
                pv = _dot((e / den).astype(BF16), vx_s[st, idx])
                out = pv if out is None else out + pv
            for p in range(pairs):
                cols = slice((grp * pairs + p) * LANES, (grp * pairs + p + 1) * LANES)
                att_s[rows, cols] = out[p * t_tile:(p + 1) * t_tile, :].astype(BF16)

    y = _dot(att_s[...], wout_ref[...])
    o_ref[...] = x + _rms(y, g_ref[1:2, :])


def _swa_decode(x, g2, w_in, w_out, widx, sink, k_cache, v_cache, n_streams):
    b, l, _ = x.shape
    assert b % n_streams == 0
    rows = n_streams * l
    s = sink.astype(F32).reshape(SWA_HKV, SWA_G // 2, 2)
    s = jnp.transpose(s, (0, 2, 1)).reshape(SWA_HKV * 2, SWA_G // 2)
    s = jnp.repeat(s, l, axis=1)
    sink_tab = jnp.broadcast_to(s[:, :, None], s.shape + (LANES,))
    row_spec = pl.BlockSpec((rows, D_MODEL), lambda i: (i, 0))
    cache_spec = pl.BlockSpec((n_streams, WINDOW, SWA_KV), lambda i: (i, 0, 0))
    kv_out = pl.BlockSpec((rows, SWA_KV), lambda i: (i, 0))
    y, k_new, v_new = pl.pallas_call(
        functools.partial(_swa_decode_kernel, t_tile=l, n_streams=n_streams),
        grid=(b // n_streams,),
        in_specs=[
            row_spec,
            _const_spec((2, D_MODEL)),
            _layer_spec(w_in, widx),
            _layer_spec(w_out, widx),
            _const_spec(sink_tab.shape),
            cache_spec, cache_spec,
        ],
        out_specs=[row_spec, kv_out, kv_out],
        out_shape=[jax.ShapeDtypeStruct((b * l, D_MODEL), F32),
                   jax.ShapeDtypeStruct((b * l, SWA_KV), F32),
                   jax.ShapeDtypeStruct((b * l, SWA_KV), F32)],
        scratch_shapes=[
            pltpu.VMEM((rows, SWA_Q), F32),
            pltpu.VMEM((n_streams, 4, WINDOW + l, LANES), BF16),
            pltpu.VMEM((n_streams, 4, WINDOW + l, LANES), BF16),
            pltpu.VMEM((rows, SWA_Q), BF16),
        ],
        compiler_params=_params(1),
        name="swa_decode",
    )(x.reshape(b * l, D_MODEL), g2, w_in, w_out, sink_tab, k_cache, v_cache)
    return y.reshape(b, l, D_MODEL), k_new, v_new


def _split3(a):
    hi = a.astype(BF16)
    r = a - hi.astype(F32)
    mid = r.astype(BF16)
    low = (r - mid.astype(F32)).astype(BF16)
    return hi, mid, low


def _hgrn_kernel(*refs, t_tile, has_state):
    if has_state:
        (x_ref, g_ref, win_ref, wout_ref, gn_ref, lb_ref, tri_ref, mask_ref, s0_ref,
         o_ref, s_ref, st_s, q_s, kp_s, fp_s, vp_s, vb_s, b_s, gate_s,
         qc1_s, kc1_s, qc2_s, kc2_s, mix_s) = refs
    else:
        (x_ref, g_ref, win_ref, wout_ref, gn_ref, lb_ref, tri_ref, mask_ref,
         o_ref, s_ref, st_s, q_s, kp_s, fp_s, vp_s, vb_s, b_s, gate_s,
         qc1_s, kc1_s, qc2_s, kc2_s, mix_s) = refs
    t = pl.program_id(1)
    pad = HG_SUB
    chunk = min(t_tile, HG_CHUNK)
    n1 = t_tile // chunk
    nb2 = chunk // HG_SUB
    body = slice(pad, pad + t_tile)

    @pl.when(t == 0)
    def _():
        for hd in range(HG_H):
            if has_state:
                st_s[hd] = s0_ref[hd].T
            else:
                st_s[hd] = jnp.zeros((HG_DV, HG_DK), F32)
        zpad = jnp.zeros((HG_H, pad, LANES), F32)
        kp_s[:, 0:pad, :] = zpad
        fp_s[:, 0:pad, :] = zpad
        vp_s[:, 0:pad, :] = zpad
        qc1_s[...] = jnp.zeros_like(qc1_s)
        kc1_s[...] = jnp.zeros_like(kc1_s)
        qc2_s[...] = jnp.zeros_like(qc2_s)
        kc2_s[...] = jnp.zeros_like(kc2_s)

    x = x_ref[...]
    h = _rms(x, g_ref[0:1, :]).astype(BF16)
    lb = lb_ref[...]
    tri = tri_ref[...]
    hpc = MXU_DIM // LANES
    for c in range(HG_QK // MXU_DIM):
        cols = slice(c * MXU_DIM, (c + 1) * MXU_DIM)

        def proj(part, cols=cols):
            return _dot(h, win_ref[:, part * HG_QK + cols.start:part * HG_QK + cols.stop])

        lbc = lb[:, cols]
        q_s[:, cols] = _silu(proj(0))
        f = proj(1)
        fgate = lbc + (1.0 - lbc) * jax.nn.sigmoid(f)
        k_all = (1.0 - lbc) * jax.nn.sigmoid(-f)
        v = proj(2)
        vb_s[:, cols] = v.astype(BF16)
        gate_s[:, cols] = proj(3)
        for i in range(hpc):
            sub = slice(i * LANES, (i + 1) * LANES)
            fp_s[c * hpc + i, body, :] = fgate[:, sub]
            kp_s[c * hpc + i, body, :] = k_all[:, sub]
            vp_s[c * hpc + i, body, :] = v[:, sub]
        hi, mid, low = _split3(jnp.log(fgate))
        b_s[:, cols] = _dot(tri, hi) + _dot(tri, mid) + _dot(tri, low)

    for hd in range(HG_H):
        ln = slice(hd * LANES, (hd + 1) * LANES)
        q = q_s[:, ln]
        k = kp_s[hd, body, :]
        bc = b_s[:, ln]
        vb = vb_s[:, ln]

        for j in range(n1 - 1):
            blk = slice(j * chunk, (j + 1) * chunk)
            after = slice((j + 1) * chunk, t_tile)
            cols = slice(j * LANES, (j + 1) * LANES)
            e_j = bc[(j + 1) * chunk - 1:(j + 1) * chunk, :]
            kc1_s[blk, cols] = (k[blk] * jnp.exp(e_j - bc[blk])).astype(BF16)
            qc1_s[after, cols] = (q[after] * jnp.exp(bc[after] - e_j)).astype(BF16)
        for c in range(n1):
            for j in range(nb2 - 1):
                r0 = c * chunk + j * HG_SUB
                blk = slice(r0, r0 + HG_SUB)
                after = slice(r0 + HG_SUB, (c + 1) * chunk)
                cols = slice(j * LANES, (j + 1) * LANES)
                e_j = bc[r0 + HG_SUB - 1:r0 + HG_SUB, :]
                kc2_s[blk, cols] = k[blk] * jnp.exp(e_j - bc[blk])
                qc2_s[after, cols] = q[after] * jnp.exp(bc[after] - e_j)

        p = _dot_nt(qc2_s[...].astype(BF16), kc2_s[...].astype(BF16)) * mask_ref[1]
        if n1 > 1:
            p = p + _dot_nt(qc1_s[...], kc1_s[...]) * mask_ref[0]
        st_old = st_s[hd]
        o = _dot(p.astype(BF16), vb) + _dot_nt((q * jnp.exp(bc)).astype(BF16), st_old.astype(BF16))

        rows = min(t_tile, HG_BAND_ROWS)
        band = []
        for r0 in range(0, t_tile, rows):
            qb = q_s[r0:r0 + rows, ln]
            acc = None
            prod = None
            for d in range(HG_SUB):
                sh = slice(pad - d + r0, pad - d + r0 + rows)
                w = qb * kp_s[hd, sh, :]
                if d >= 1:
                    fsh = fp_s[hd, pad - d + 1 + r0:pad - d + 1 + r0 + rows, :]
                    prod = fsh if d == 1 else prod * fsh
                    w = w * prod
                term = jnp.sum(w, axis=-1, keepdims=True) * vp_s[hd, sh, :]
                acc = term if acc is None else acc + term
            band.append(acc)
        o = o + jnp.concatenate(band, axis=0)

        b_last = bc[t_tile - 1:t_tile, :]
        k_end = (k * jnp.exp(b_last - bc)).astype(BF16)
        st_new = st_old * jnp.exp(b_last) + _dot_tn(vb, k_end)
        st_s[hd] = st_new
        s_ref[hd] = st_new.T

        o = _rms(o, gn_ref[:, ln])
        mix_s[:, ln] = (_silu(gate_s[:, ln]) * o).astype(BF16)

    y = _dot(mix_s[...], wout_ref[...])
    o_ref[...] = x + _rms(y, g_ref[1:2, :])


def _hgrn(x, g2, w_in, w_out, widx, gn, lb, s0, t_tile):
    b, l, _ = x.shape
    has_state = s0 is not None
    chunk = min(t_tile, HG_CHUNK)
    n1 = t_tile // chunk
    nb2 = chunk // HG_SUB
    tri = jnp.asarray(np.tril(np.ones((t_tile, t_tile), np.float32)), dtype=BF16)
    n = np.arange(t_tile)
    far = (n[:, None] - n[None, :]) >= HG_SUB
    same = (n[:, None] // chunk) == (n[None, :] // chunk)
    masks = np.stack([far, far & same]).astype(np.float32)
    state_spec = pl.BlockSpec((None, HG_H, HG_DK, HG_DV), lambda i, j: (i, 0, 0, 0))
    in_specs = [
        pl.BlockSpec((None, t_tile, D_MODEL), lambda i, j: (i, j, 0)),
        _const_spec((2, D_MODEL)),
        _layer_spec(w_in, widx),
        _layer_spec(w_out, widx),
        _const_spec((1, HG_V)),
        _const_spec((1, HG_QK)),
        _const_spec(tri.shape),
        _const_spec(masks.shape),
    ]
    args = [x, g2, w_in, w_out, gn, lb, tri, masks]
    if has_state:
        in_specs.append(state_spec)
        args.append(s0)
    cols1 = max(n1 - 1, 1) * LANES
    cols2 = (nb2 - 1) * LANES
    return pl.pallas_call(
        functools.partial(_hgrn_kernel, t_tile=t_tile, has_state=has_state),
        grid=(b, l // t_tile),
        in_specs=in_specs,
        out_specs=[pl.BlockSpec((None, t_tile, D_MODEL), lambda i, j: (i, j, 0)), state_spec],
        out_shape=[jax.ShapeDtypeStruct(x.shape, F32),
                   jax.ShapeDtypeStruct((b, HG_H, HG_DK, HG_DV), F32)],
        scratch_shapes=[
            pltpu.VMEM((HG_H, HG_DV, HG_DK), F32),
            pltpu.VMEM((t_tile, HG_QK), F32),
            pltpu.VMEM((HG_H, HG_SUB + t_tile, LANES), F32),
            pltpu.VMEM((HG_H, HG_SUB + t_tile, LANES), F32),
            pltpu.VMEM((HG_H, HG_SUB + t_tile, LANES), F32),
            pltpu.VMEM((t_tile, HG_V), BF16),
            pltpu.VMEM((t_tile, HG_QK), F32),
            pltpu.VMEM((t_tile, HG_V), F32),
            pltpu.VMEM((t_tile, cols1), BF16),
            pltpu.VMEM((t_tile, cols1), BF16),
            pltpu.VMEM((t_tile, cols2), F32),
            pltpu.VMEM((t_tile, cols2), F32),
            pltpu.VMEM((t_tile, HG_V), BF16),
        ],
        compiler_params=_params(2),
        name="hgrn2",
    )(*args)


def _rope_table(pos0, length):
    half = RET_DK // 2
    inv = ROPE_BASE ** (-np.arange(half, dtype=np.float64) / half)
    pos = pos0 + np.arange(length, dtype=np.float64)
    ang = pos[:, None] * inv[None, :]
    cos = np.cos(ang)
    sin = np.sin(ang)
    c2 = np.concatenate([cos, cos], axis=-1)
    s2 = np.concatenate([-sin, sin], axis=-1)
    ks = RET_DK ** -0.5
    return np.stack([c2, s2, c2 * ks, s2 * ks]).astype(np.float32)


def _ret_log_decay():
    return np.log1p(-(2.0 ** (-5.0 - np.arange(RET_H, dtype=np.float64))))


def _ret_decay_matrix(t_tile, chunk):
    lg = _ret_log_decay()
    n = np.arange(t_tile)
    cn = n // chunk
    diff = (n[:, None] - n[None, :]).astype(np.float64)
    same = cn[:, None] == cn[None, :]
    later = cn[:, None] > cn[None, :]
    expo = np.where(same, np.abs(diff), diff)
    dec = np.exp(lg[:, None, None] * expo[None])
    return np.where((same | later)[None], dec, 0.0).astype(np.float32)


def _lower_bounds(lb_param):
    c = jnp.cumsum(jax.nn.softmax(lb_param.astype(F32), axis=0), axis=0)
    return c - c[0]


def kernel(x_prompt, x_sample, state_ret, cache_swa_k, cache_swa_v, state_hgrn, norm_g, w_ff_in, w_ff_out,
           ret_w_in, ret_w_out, ret_gn_g, swa_w_in, swa_w_out, swa_sink, hg_w_in, hg_w_out, hg_gn_g, hg_lb):
    bf = lambda w: w.astype(BF16)
    w_ff_in, w_ff_out = bf(w_ff_in), bf(w_ff_out)
    ret_w_in, ret_w_out = bf(ret_w_in), bf(ret_w_out)
    swa_w_in, swa_w_out = bf(swa_w_in), bf(swa_w_out)
    hg_w_in, hg_w_out = bf(hg_w_in), bf(hg_w_out)
    lbs = _lower_bounds(hg_lb)

    bp, lp, _ = x_prompt.shape
    bs, ls, _ = x_sample.shape
    xp = x_prompt.reshape(bp * lp, D_MODEL)
    xs = x_sample.reshape(bs * ls, D_MODEL)
    rope_p, rope_s = _rope_table(0.0, lp), _rope_table(float(PAST_LEN), ls)
    dmat_p, dmat_s = _ret_decay_matrix(MIX_TOKENS, CHUNK), _ret_decay_matrix(ls, ls)
    lg_tab = np.broadcast_to(_ret_log_decay().astype(np.float32)[:, None, None], (RET_H, 1, RET_DV))
    ret_p, ret_s = None, None
    k_p, v_p, k_s, v_s, hg_p, hg_s = [], [], [], [], [], []

    for li in range(DEPTH):
        kind, j = li % 3, li // 3
        xp = _ffn(xp, norm_g[li, 0:2], w_ff_in, w_ff_out, (li, 0))
        xs = _ffn(xs, norm_g[li, 0:2], w_ff_in, w_ff_out, (li, 0))
        xp3, xs3 = xp.reshape(bp, lp, D_MODEL), xs.reshape(bs, ls, D_MODEL)
        g2 = norm_g[li, 2:4]
        if kind == 0:
            gn = ret_gn_g[j][None, :]
            xp3, ret_p = _retention(xp3, g2, ret_w_in, ret_w_out, (j,), gn, rope_p, dmat_p, lg_tab, None,
                                    MIX_TOKENS, 1, RET_SUBTILES, ret_p)
            xs3, ret_s = _retention(xs3, g2, ret_w_in, ret_w_out, (j,), gn, rope_s, dmat_s, lg_tab,
                                    state_ret, ls, DEC_STREAMS, 1, ret_s)
        elif kind == 1:
            xp3, kn, vn = _swa_prompt(xp3, g2, swa_w_in, swa_w_out, (j,), swa_sink[j], SWA_TOKENS)
            k_p.append(kn.reshape(bp, WINDOW, SWA_HKV, SWA_DH))
            v_p.append(vn.reshape(bp, WINDOW, SWA_HKV, SWA_DH))
            kc = cache_swa_k[j].reshape(bs, WINDOW, SWA_KV)
            vc = cache_swa_v[j].reshape(bs, WINDOW, SWA_KV)
            xs3, kn, vn = _swa_decode(xs3, g2, swa_w_in, swa_w_out, (j,), swa_sink[j], kc, vc, DEC_SWA_STREAMS)
            k_s.append(kn.reshape(bs, ls, SWA_HKV, SWA_DH))
            v_s.append(vn.reshape(bs, ls, SWA_HKV, SWA_DH))
        else:
            gn, lb = hg_gn_g[j][None, :], lbs[li][None, :]
            xp3, sp = _hgrn(xp3, g2, hg_w_in, hg_w_out, (j,), gn, lb, None, MIX_TOKENS)
            xs3, ss = _hgrn(xs3, g2, hg_w_in, hg_w_out, (j,), gn, lb, state_hgrn[j], ls)
            hg_p.append(sp)
            hg_s.append(ss)
        xp, xs = xp3.reshape(bp * lp, D_MODEL), xs3.reshape(bs * ls, D_MODEL)
        xp = _ffn(xp, norm_g[li, 4:6], w_ff_in, w_ff_out, (li, 1))
        xs = _ffn(xs, norm_g[li, 4:6], w_ff_in, w_ff_out, (li, 1))

    return (xp.reshape(bp, lp, D_MODEL), xs.reshape(bs, ls, D_MODEL), ret_p, ret_s,
            jnp.stack(k_p), jnp.stack(v_p), jnp.stack(k_s), jnp.stack(v_s), jnp.stack(hg_p), jnp.stack(hg_s))
```
